```python
import jax, jax.numpy as jnp
from jax import lax
import numpy as np

D_MODEL = 2048
BATCH = 16
SEQ = 2048
DEPTH = 1

CHUNK = 64
N_META = 16
LEAD_PAD = CHUNK - N_META
Q_BLOCK = 128

MIX_WIDTH = D_MODEL
RET_WIDTH = MIX_WIDTH // 2
MLA_WIDTH = MIX_WIDTH - RET_WIDTH

RET_HEADS = 8
RET_HEAD_DIM = RET_WIDTH // RET_HEADS

MLA_HEADS = 8
MLA_V_DIM = MLA_WIDTH // MLA_HEADS
MLA_NOPE_DIM = 128
MLA_ROPE_DIM = 64
Q_LORA = 512
KV_LORA = 256
MLA_QK_DIM = MLA_NOPE_DIM + MLA_ROPE_DIM

D_FF = 4 * D_MODEL

ROPE_BASE = 10000.0
EPS = 1e-6

IN_SIZES = [RET_WIDTH, RET_WIDTH, RET_WIDTH, RET_WIDTH, Q_LORA, KV_LORA, MLA_ROPE_DIM]
IN_WIDTH = int(sum(IN_SIZES))
IN_SPLITS = [int(v) for v in np.cumsum(IN_SIZES)[:-1]]

kernel_name = "hymba_retention_mla_streaming_block"


def rmsnorm(x, g):
    xf = x.astype(jnp.float32)
    y = xf * lax.rsqrt(jnp.mean(xf * xf, axis=-1, keepdims=True) + EPS)
    return (y * g.astype(jnp.float32)).astype(x.dtype)


def rope_tables(pos, dim):
    inv = 1.0 / (ROPE_BASE ** (jnp.arange(0, dim, 2, dtype=jnp.float32) / dim))
    ang = pos[:, None] * inv[None, :]
    return jnp.cos(ang), jnp.sin(ang)


def apply_rope(x, cos, sin):
    half = x.shape[-1] // 2
    cos = cos.astype(x.dtype)
    sin = sin.astype(x.dtype)
    x1, x2 = x[..., :half], x[..., half:]
    return jnp.concatenate([x1 * cos - x2 * sin, x1 * sin + x2 * cos], axis=-1)


def retention_chunkwise(q, k, v):
    B, L, H, dk = q.shape
    dv = v.shape[-1]
    nc = L // CHUNK
    dt = q.dtype
    log_gamma = jnp.log(1.0 - 2.0 ** (-5.0 - jnp.arange(H, dtype=jnp.float32)))
    q = (q * (dk ** -0.5)).reshape(B, nc, CHUNK, H, dk)
    k = k.reshape(B, nc, CHUNK, H, dk)
    v = v.reshape(B, nc, CHUNK, H, dv)
    idx = jnp.arange(CHUNK, dtype=jnp.float32)
    dist = jnp.abs(idx[:, None] - idx[None, :])
    d_intra = jnp.exp(log_gamma[:, None, None] * dist[None]).astype(dt)
    scores = jnp.einsum('bnihd,bnjhd->bnhij', q, k) * d_intra
    o_intra = jnp.einsum('bnhij,bnjhe->bnihe', scores, v)
    w_k = jnp.exp(log_gamma[None, :] * (CHUNK - 1.0 - idx)[:, None]).astype(dt)
    kv = jnp.einsum('bnjhd,jh,bnjhe->bnhde', k, w_k, v)
    chunk_decay = jnp.exp(log_gamma * CHUNK).astype(kv.dtype)[None, :, None, None]

    def step(state, kv_c):
        return state * chunk_decay + kv_c, state

    _, states = lax.scan(step, jnp.zeros((B, H, dk, dv), kv.dtype), jnp.moveaxis(kv, 1, 0))
    states = jnp.moveaxis(states, 0, 1)
    w_q = jnp.exp(log_gamma[None, :] * (idx + 1.0)[:, None]).astype(dt)
    o_inter = jnp.einsum('bnihd,ih,bnhde->bnihe', q, w_q, states)
    return (o_intra + o_inter).reshape(B, L, H, dv)


def head_groupnorm(o, g):
    of = o.astype(jnp.float32)
    mu = jnp.mean(of, axis=-1, keepdims=True)
    var = jnp.mean(jnp.square(of - mu), axis=-1, keepdims=True)
    y = ((of - mu) * lax.rsqrt(var + EPS)).reshape(o.shape[0], o.shape[1], -1)
    return (y * g.astype(jnp.float32)).astype(o.dtype)


def mla_attention(q_lat, q_rope, c_kv, k_rope, chunk_id, key_valid):
    B, L, H, C = q_lat.shape
    nqb = L // Q_BLOCK
    scale = MLA_QK_DIM ** -0.5
    neg = jnp.finfo(jnp.float32).min
    ql_b = jnp.moveaxis(q_lat.reshape(B, nqb, Q_BLOCK, H, C), 1, 0)
    qr_b = jnp.moveaxis(q_rope.reshape(B, nqb, Q_BLOCK, H, q_rope.shape[-1]), 1, 0)
    qc_b = chunk_id.reshape(nqb, Q_BLOCK)

    def attend(args):
        ql, qr, qc = args
        s = (jnp.einsum('bqhc,bkc->bhqk', ql, c_kv)
             + jnp.einsum('bqhr,bkr->bhqk', qr, k_rope)).astype(jnp.float32) * scale
        mask = (chunk_id[None, :] <= qc[:, None]) & key_valid[None, :]
        s = jnp.where(mask[None, None], s, neg)
        p = jax.nn.softmax(s, axis=-1).astype(c_kv.dtype)
        return jnp.einsum('bhqk,bkc->bqhc', p, c_kv)

    o = lax.map(attend, (ql_b, qr_b, qc_b))
    return jnp.moveaxis(o, 0, 1).reshape(B, L, H, C)


def setup_inputs(seed: int = 0) -> dict:
    key = jax.random.key(seed)
    ks = jax.random.split(key, 16)
    f32 = jnp.float32

    def w(k, shape, fan_in):
        return jax.random.normal(k, shape, f32) * (fan_in ** -0.5)

    def gain(k, shape):
        return 1.0 + 0.02 * jax.random.normal(k, shape, f32)

    return {
        "x": jax.random.normal(ks[0], (BATCH, SEQ, D_MODEL), f32),
        "meta_tokens": jax.random.normal(ks[1], (N_META, D_MODEL), f32),
        "norm_mix_g": gain(ks[2], (DEPTH, D_MODEL)),
        "w_in": w(ks[3], (DEPTH, D_MODEL, IN_WIDTH), D_MODEL),
        "ret_out_g": gain(ks[4], (DEPTH, RET_WIDTH)),
        "q_norm_g": gain(ks[5], (DEPTH, Q_LORA)),
        "w_uq": w(ks[6], (DEPTH, Q_LORA, MLA_HEADS * MLA_QK_DIM), Q_LORA),
        "kv_norm_g": gain(ks[7], (DEPTH, KV_LORA)),
        "w_uk": w(ks[8], (DEPTH, KV_LORA, MLA_HEADS * MLA_NOPE_DIM), KV_LORA),
        "w_uv": w(ks[9], (DEPTH, KV_LORA, MLA_HEADS * MLA_V_DIM), KV_LORA),
        "mla_out_g": gain(ks[10], (DEPTH, MLA_WIDTH)),
        "w_o": w(ks[11], (DEPTH, MIX_WIDTH, D_MODEL), MIX_WIDTH),
        "norm_mlp_g": gain(ks[12], (DEPTH, D_MODEL)),
        "w_up": w(ks[13], (DEPTH, D_MODEL, D_FF), D_MODEL),
        "w_down": w(ks[14], (DEPTH, D_FF, D_MODEL), D_FF),
        "final_norm_g": gain(ks[15], (D_MODEL,)),
    }


def reference(x, meta_tokens, norm_mix_g, w_in, ret_out_g, q_norm_g, w_uq, kv_norm_g,
              w_uk, w_uv, mla_out_g, w_o, norm_mlp_g, w_up, w_down, final_norm_g):
    B, S, D = x.shape
    dt = x.dtype
    tail = (-(CHUNK + S)) % Q_BLOCK
    L = CHUNK + S + tail
    h = jnp.concatenate([
        jnp.zeros((B, LEAD_PAD, D), dt),
        jnp.broadcast_to(meta_tokens.astype(dt)[None], (B, N_META, D)),
        x,
        jnp.zeros((B, tail, D), dt),
    ], axis=1)

    p = jnp.arange(L)
    chunk_id = p // CHUNK
    key_valid = p >= LEAD_PAD
    valid_f = key_valid.astype(dt)[None, :, None, None]
    pos = jnp.maximum(p - LEAD_PAD, 0).astype(jnp.float32)
    cos_r, sin_r = rope_tables(pos, RET_HEAD_DIM)
    cos_m, sin_m = rope_tables(pos, MLA_ROPE_DIM)

    for l in range(DEPTH):
        u = rmsnorm(h, norm_mix_g[l])
        proj = u @ w_in[l]
        rq, rk, rv, rg, cq, ckv, kr = jnp.split(proj, IN_SPLITS, axis=-1)

        rq = apply_rope(rq.reshape(B, L, RET_HEADS, RET_HEAD_DIM), cos_r[:, None, :], sin_r[:, None, :])
        rk = apply_rope(rk.reshape(B, L, RET_HEADS, RET_HEAD_DIM), cos_r[:, None, :], sin_r[:, None, :]) * valid_f
        rv = rv.reshape(B, L, RET_HEADS, RET_HEAD_DIM)
        ret = retention_chunkwise(rq, rk, rv)
        ret_out = jax.nn.silu(rg) * head_groupnorm(ret, ret_out_g[l])

        c_q = rmsnorm(cq, q_norm_g[l])
        q = (c_q @ w_uq[l]).reshape(B, L, MLA_HEADS, MLA_QK_DIM)
        q_nope = q[..., :MLA_NOPE_DIM]
        q_rope = apply_rope(q[..., MLA_NOPE_DIM:], cos_m[:, None, :], sin_m[:, None, :])
        c_kv = rmsnorm(ckv, kv_norm_g[l])
        k_rope = apply_rope(kr, cos_m, sin_m)
        w_uk_h = w_uk[l].reshape(KV_LORA, MLA_HEADS, MLA_NOPE_DIM)
        w_uv_h = w_uv[l].reshape(KV_LORA, MLA_HEADS, MLA_V_DIM)
        q_lat = jnp.einsum('blhd,chd->blhc', q_nope, w_uk_h)
        o_lat = mla_attention(q_lat, q_rope, c_kv, k_rope, chunk_id, key_valid)
        mla = jnp.einsum('blhc,chv->blhv', o_lat, w_uv_h).reshape(B, L, MLA_WIDTH)
        mla_out = rmsnorm(mla, mla_out_g[l])

        h = h + jnp.concatenate([ret_out, mla_out], axis=-1) @ w_o[l]

        m = rmsnorm(h, norm_mlp_g[l])
        h = h + jnp.square(jax.nn.relu(m @ w_up[l])) @ w_down[l]

    return rmsnorm(h[:, CHUNK:CHUNK + S], final_norm_g)
```

```python
import functools

import numpy as np
import jax
import jax.numpy as jnp
from jax import lax
from jax.experimental import pallas as pl
from jax.experimental.pallas import tpu as pltpu

F32 = jnp.float32
BF16 = jnp.bfloat16

CHUNK = 64
N_META = 16
RET_HEADS = 8
RET_HEAD_DIM = 128
MLA_HEADS = 8
MLA_V_DIM = 128
MLA_NOPE_DIM = 128
MLA_ROPE_DIM = 64
Q_LORA = 512
KV_LORA = 256
MLA_QK_DIM = MLA_NOPE_DIM + MLA_ROPE_DIM
ROPE_BASE = 10000.0
EPS = 1e-6

LANES = 128
RET_WIDTH = RET_HEADS * RET_HEAD_DIM
MLA_WIDTH = MLA_HEADS * MLA_V_DIM
COL_CQ = 4 * RET_WIDTH
COL_CKV = COL_CQ + Q_LORA
COL_KR = COL_CKV + KV_LORA
PROJ_WIDTH = 5120
KAUG = KV_LORA + LANES

VMEM_LIMIT_CAP = 60000 * 1024


def _vmem_limit(nbytes):
    return int(min(VMEM_LIMIT_CAP, nbytes))


def _rms(x, g):
    ms = jnp.mean(x * x, axis=-1, keepdims=True)
    return x * lax.rsqrt(ms + EPS) * g


def _inproj_kernel(x_ref, g_ref, w_ref, o_ref, u_ref):
    @pl.when(pl.program_id(1) == 0)
    def _():
        u_ref[...] = _rms(x_ref[...], g_ref[...]).astype(BF16)

    o_ref[...] = jnp.dot(u_ref[...], w_ref[...],
                         preferred_element_type=F32).astype(o_ref.dtype)


def _inproj(x, g, w, tm, tn):
    m, d = x.shape
    n = w.shape[1]
    est = 2 * tm * d * 4 + tm * d * 2 + 2 * d * tn * 2 + 2 * tm * tn * 2 + tm * tn * 4 + tm * d * 4
    return pl.pallas_call(
        _inproj_kernel,
        grid=(m // tm, n // tn),
        in_specs=[
            pl.BlockSpec((tm, d), lambda i, j: (i, 0)),
            pl.BlockSpec((1, d), lambda i, j: (0, 0)),
            pl.BlockSpec((d, tn), lambda i, j: (0, j)),
        ],
        out_specs=pl.BlockSpec((tm, tn), lambda i, j: (i, j)),
        out_shape=jax.ShapeDtypeStruct((m, n), BF16),
        scratch_shapes=[pltpu.VMEM((tm, d), BF16)],
        compiler_params=pltpu.CompilerParams(
            dimension_semantics=("arbitrary", "arbitrary"),
            vmem_limit_bytes=_vmem_limit(est + (8 << 20))),
        name="inproj",
    )(x, g, w)


def _rope128(x, cos2, sin2):
    return x * cos2 + pltpu.roll(x, 64, 1) * sin2


def _retention_kernel(q_ref, k_ref, v_ref, g_ref, cos_ref, sin_ref,
                      km_ref, vm_ref, cosm_ref, sinm_ref, gout_ref,
                      o_ref,
                      state_ref, d_ref, wq_ref, wk_ref, *, T):
    b = pl.program_id(0)
    t = pl.program_id(1)
    H = RET_HEADS
    dk = RET_HEAD_DIM

    def log_gamma(h):
        e = jnp.full((1, 1), -5.0 - h, F32)
        return jnp.log(1.0 - jnp.exp2(e))

    @pl.when((b == 0) & (t == 0))
    def _():
        ii = lax.broadcasted_iota(jnp.int32, (T, T), 0)
        jj = lax.broadcasted_iota(jnp.int32, (T, T), 1)
        dist = jnp.abs(ii - jj).astype(F32)
        vis = (jj // CHUNK) <= (ii // CHUNK)
        row = lax.broadcasted_iota(jnp.int32, (T, LANES), 0).astype(F32)
        for h in range(H):
            lg = log_gamma(h)
            d_ref[h] = jnp.where(vis, jnp.exp(lg * dist), 0.0)
            wq_ref[h] = jnp.exp(lg * (row + 1.0))
            wk_ref[h] = jnp.exp(lg * (T - 1.0 - row))

    @pl.when(t == 0)
    def _():
        rowm = lax.broadcasted_iota(jnp.int32, (N_META, LANES), 0).astype(F32)
        cm = cosm_ref[...]
        sm = sinm_ref[...]
        for h in range(H):
            lg = log_gamma(h)
            km = _rope128(km_ref[:, h * dk:(h + 1) * dk].astype(F32), cm, sm)
            km = km * jnp.exp(lg * (N_META - 1.0 - rowm))
            vm = vm_ref[:, h * dk:(h + 1) * dk]
            state_ref[h] = lax.dot_general(
                km.astype(BF16), vm, (((0,), (0,)), ((), ())),
                preferred_element_type=F32)

    cos2 = cos_ref[...]
    sin2 = sin_ref[...]
    scale = dk ** -0.5
    for h in range(H):
        cs = slice(h * dk, (h + 1) * dk)
        lg = log_gamma(h)
        q = _rope128(q_ref[0, :, cs].astype(F32), cos2, sin2) * scale
        k = _rope128(k_ref[0, :, cs].astype(F32), cos2, sin2)
        v = v_ref[0, :, cs]
        s = lax.dot_general(q.astype(BF16), k.astype(BF16),
                            (((1,), (1,)), ((), ())), preferred_element_type=F32)
        sd = (s * d_ref[h]).astype(BF16)
        st = state_ref[h]
        o = jnp.dot(sd, v, preferred_element_type=F32)
        o = o + jnp.dot((q * wq_ref[h]).astype(BF16), st.astype(BF16),
                        preferred_element_type=F32)
        kv = lax.dot_general((k * wk_ref[h]).astype(BF16), v,
                             (((0,), (0,)), ((), ())), preferred_element_type=F32)
        state_ref[h] = st * jnp.exp(lg * float(T)) + kv
        mu = jnp.mean(o, axis=-1, keepdims=True)
        oc = o - mu
        var = jnp.mean(oc * oc, axis=-1, keepdims=True)
        y = oc * lax.rsqrt(var + EPS) * gout_ref[:, cs]
        gate = g_ref[0, :, cs].astype(F32)
        gate = gate * (1.0 / (1.0 + jnp.exp(-gate)))
        o_ref[0, :, cs] = (gate * y).astype(o_ref.dtype)


def _retention(projv, proj_meta, cos2, sin2, cosm, sinm, gout, T):
    B, S, _ = projv.shape
    W = RET_WIDTH
    est = (2 * 4 * T * W * 2 + 2 * T * W * 2 + 4 * T * LANES * 4
           + RET_HEADS * (T * T + 2 * T * LANES + 128 * 128) * 4 + 16 * T * T * 4)
    kern = functools.partial(_retention_kernel, T=T)
    return pl.pallas_call(
        kern,
        grid=(B, S // T),
        in_specs=[
            pl.BlockSpec((1, T, W), lambda b, t: (b, t, 0)),
            pl.BlockSpec((1, T, W), lambda b, t: (b, t, 1)),
            pl.BlockSpec((1, T, W), lambda b, t: (b, t, 2)),
            pl.BlockSpec((1, T, W), lambda b, t: (b, t, 3)),
            pl.BlockSpec((T, LANES), lambda b, t: (t, 0)),
            pl.BlockSpec((T, LANES), lambda b, t: (t, 0)),
            pl.BlockSpec((N_META, W), lambda b, t: (0, 1)),
            pl.BlockSpec((N_META, W), lambda b, t: (0, 2)),
            pl.BlockSpec((N_META, LANES), lambda b, t: (0, 0)),
            pl.BlockSpec((N_META, LANES), lambda b, t: (0, 0)),
            pl.BlockSpec((1, W), lambda b, t: (0, 0)),
        ],
        out_specs=pl.BlockSpec((1, T, W), lambda b, t: (b, t, 0)),
        out_shape=jax.ShapeDtypeStruct((B, S, W), BF16),
        scratch_shapes=[
            pltpu.VMEM((RET_HEADS, RET_HEAD_DIM, RET_HEAD_DIM), F32),
            pltpu.VMEM((RET_HEADS, T, T), F32),
            pltpu.VMEM((RET_HEADS, T, LANES), F32),
            pltpu.VMEM((RET_HEADS, T, LANES), F32),
        ],
        compiler_params=pltpu.CompilerParams(
            dimension_semantics=("arbitrary", "arbitrary"),
            vmem_limit_bytes=_vmem_limit(est + (8 << 20))),
        name="retention",
    )(projv, projv, projv, projv, cos2, sin2, proj_meta, proj_meta, cosm, sinm, gout)


def _swap32(x):
    lane = lax.broadcasted_iota(jnp.int32, x.shape, 1)
    return jnp.where((lane % 64) < 32, pltpu.roll(x, 96, 1), pltpu.roll(x, 32, 1))


def _rope64x2(x, pc, ps):
    return x * pc + _swap32(x) * ps


def _mla_kernel(cq_ref, kv_ref, kvm_ref, pc_ref, ps_ref, pcm_ref, psm_ref,
                qg_ref, kvg_ref, wuq_ref, wukt_ref, wuv_ref, og_ref,
                o_ref,
                kaug_ref, qaug_ref, p_ref, m_ref, l_ref, a_ref, acc_ref,
                *, TQ, TK, S):
    i = pl.program_id(1)
    H = MLA_HEADS
    R = H * TQ
    scale = MLA_QK_DIM ** -0.5
    NEG = -1e30

    def prep_keys(kv, pc, ps):
        c = _rms(kv[:, :KV_LORA], kvg_ref[...])
        kr = _rope64x2(kv[:, KV_LORA:], pc, ps)
        return c.astype(BF16), kr.astype(BF16)

    @pl.when(i == 0)
    def _():
        kaug_ref[0:TK, :] = jnp.zeros((TK, KAUG), BF16)
        cm, krm = prep_keys(kvm_ref[...].astype(F32), pcm_ref[...], psm_ref[...])
        kaug_ref[0:N_META, 0:KV_LORA] = cm
        kaug_ref[0:N_META, KV_LORA:] = krm
        c, kr = prep_keys(kv_ref[0].astype(F32), pc_ref[...], ps_ref[...])
        kaug_ref[TK:, 0:KV_LORA] = c
        kaug_ref[TK:, KV_LORA:] = kr

    q0 = pl.multiple_of(i * TQ, TQ)
    cq = _rms(cq_ref[0].astype(F32), qg_ref[...]).astype(BF16)
    q = jnp.dot(cq, wuq_ref[...], preferred_element_type=F32)
    pcq = pc_ref[pl.ds(q0, TQ), :]
    psq = ps_ref[pl.ds(q0, TQ), :]
    lane = lax.broadcasted_iota(jnp.int32, (TQ, LANES), 1)
    for h in range(H):
        qn = q[:, h * MLA_NOPE_DIM:(h + 1) * MLA_NOPE_DIM].astype(BF16)
        ql = jnp.dot(qn, wukt_ref[h], preferred_element_type=F32) * scale
        qaug_ref[h * TQ:(h + 1) * TQ, 0:KV_LORA] = ql.astype(BF16)
    for p in range(H // 2):
        c0 = H * MLA_NOPE_DIM + p * LANES
        rp = _rope64x2(q[:, c0:c0 + LANES], pcq, psq) * scale
        qaug_ref[(2 * p) * TQ:(2 * p + 1) * TQ, KV_LORA:] = jnp.where(lane < 64, rp, 0.0).astype(BF16)
        qaug_ref[(2 * p + 1) * TQ:(2 * p + 2) * TQ, KV_LORA:] = jnp.where(lane >= 64, rp, 0.0).astype(BF16)

    def scores(k0):
        kb = kaug_ref[pl.ds(k0, TK), :]
        s = lax.dot_general(qaug_ref[...], kb, (((1,), (1,)), ((), ())),
                            preferred_element_type=F32)
        return s, kb[:, 0:KV_LORA]

    s_all, vb = scores(0)
    colh = lax.broadcasted_iota(jnp.int32, (TQ, TK), 1)
    for h in range(H):
        rs = slice(h * TQ, (h + 1) * TQ)
        s = jnp.where(colh < N_META, s_all[rs], NEG)
        m = jnp.max(s, axis=-1, keepdims=True)
        p = jnp.exp(s - m)
        m_ref[rs] = m
        l_ref[rs] = jnp.sum(p, axis=-1, keepdims=True)
        p_ref[rs] = p.astype(BF16)
    acc_ref[...] = jnp.dot(p_ref[...], vb, preferred_element_type=F32)

    def block(k0, bias):
        s_all, vb = scores(k0)
        for h in range(H):
            rs = slice(h * TQ, (h + 1) * TQ)
            s = s_all[rs]
            if bias is not None:
                s = s + bias
            m_prev = m_ref[rs]
            m_new = jnp.maximum(m_prev, jnp.max(s, axis=-1, keepdims=True))
            alpha = jnp.exp(m_prev - m_new)
            p = jnp.exp(s - m_new)
            l_ref[rs] = alpha * l_ref[rs] + jnp.sum(p, axis=-1, keepdims=True)
            m_ref[rs] = m_new
            a_ref[rs] = alpha
            p_ref[rs] = p.astype(BF16)
        acc_ref[...] = a_ref[...] * acc_ref[...] + jnp.dot(
            p_ref[...], vb, preferred_element_type=F32)

    def body(kb_idx, carry):
        block(pl.multiple_of((kb_idx + 1) * TK, TK), None)
        return carry

    lax.fori_loop(0, i, body, 0)

    rq = lax.broadcasted_iota(jnp.int32, (TQ, TK), 0) // CHUNK
    ck = lax.broadcasted_iota(jnp.int32, (TQ, TK), 1) // CHUNK
    bias = jnp.where(ck <= rq, 0.0, NEG).astype(F32)
    block(pl.multiple_of((i + 1) * TK, TK), bias)

    outs = []
    for h in range(H):
        rs = slice(h * TQ, (h + 1) * TQ)
        o = (acc_ref[rs] / l_ref[rs]).astype(BF16)
        outs.append(jnp.dot(o, wuv_ref[h], preferred_element_type=F32))
    mla = jnp.concatenate(outs, axis=-1)
    o_ref[0] = _rms(mla, og_ref[...]).astype(o_ref.dtype)


def _mla(projv, proj_meta, pc, ps, pcm, psm, qg, kvg, wuq, wukt, wuv, og, TQ):
    B, S, _ = projv.shape
    TK = TQ
    R = MLA_HEADS * TQ
    nq = wuq.shape[1]
    est = (2 * TQ * Q_LORA * 2 + 2 * S * KAUG * 2 + 4 * S * LANES * 4
           + 2 * (Q_LORA * nq + 2 * MLA_HEADS * 128 * KV_LORA) * 2 + 2 * TQ * MLA_WIDTH * 2
           + (TK + S) * KAUG * 2 + R * KAUG * 2 + R * TK * 2 + 3 * R * LANES * 4 + R * KV_LORA * 4
           + 3 * R * TK * 4 + S * KAUG * 4 * 2)
    kern = functools.partial(_mla_kernel, TQ=TQ, TK=TK, S=S)
    const2 = lambda b, i: (0, 0)
    const3 = lambda b, i: (0, 0, 0)
    return pl.pallas_call(
        kern,
        grid=(B, S // TQ),
        in_specs=[
            pl.BlockSpec((1, TQ, Q_LORA), lambda b, i: (b, i, COL_CQ // Q_LORA)),
            pl.BlockSpec((1, S, KAUG), lambda b, i: (b, 0, COL_CKV // KAUG)),
            pl.BlockSpec((N_META, KAUG), lambda b, i: (0, COL_CKV // KAUG)),
            pl.BlockSpec((S, LANES), const2),
            pl.BlockSpec((S, LANES), const2),
            pl.BlockSpec((N_META, LANES), const2),
            pl.BlockSpec((N_META, LANES), const2),
            pl.BlockSpec((1, Q_LORA), const2),
            pl.BlockSpec((1, KV_LORA), const2),
            pl.BlockSpec((Q_LORA, nq), const2),
            pl.BlockSpec((MLA_HEADS, MLA_NOPE_DIM, KV_LORA), const3),
            pl.BlockSpec((MLA_HEADS, KV_LORA, MLA_V_DIM), const3),
            pl.BlockSpec((1, MLA_WIDTH), const2),
        ],
        out_specs=pl.BlockSpec((1, TQ, MLA_WIDTH), lambda b, i: (b, i, 0)),
        out_shape=jax.ShapeDtypeStruct((B, S, MLA_WIDTH), BF16),
        scratch_shapes=[
            pltpu.VMEM((TK + S, KAUG), BF16),
            pltpu.VMEM((R, KAUG), BF16),
            pltpu.VMEM((R, TK), BF16),
            pltpu.VMEM((R, 1), F32),
            pltpu.VMEM((R, 1), F32),
            pltpu.VMEM((R, 1), F32),
            pltpu.VMEM((R, KV_LORA), F32),
        ],
        compiler_params=pltpu.CompilerParams(
            dimension_semantics=("arbitrary", "arbitrary"),
            vmem_limit_bytes=_vmem_limit(est + (8 << 20))),
        name="mla",
    )(projv, projv, proj_meta, pc, ps, pcm, psm, qg, kvg, wuq, wukt, wuv, og)


def _oproj_kernel(x_ref, a1_ref, a2_ref, w1_ref, w2_ref, o_ref):
    acc = jnp.dot(a1_ref[...], w1_ref[...], preferred_element_type=F32)
    acc = acc + jnp.dot(a2_ref[...], w2_ref[...], preferred_element_type=F32)
    o_ref[...] = x_ref[...] + acc


def _oproj(x, a1, a2, w1, w2, tm):
    m, d = x.shape
    k = a1.shape[1]
    est = 2 * tm * d * 4 * 2 + 2 * 2 * tm * k * 2 + 2 * 2 * k * d * 2 + tm * d * 4
    return pl.pallas_call(
        _oproj_kernel,
        grid=(m // tm,),
        in_specs=[
            pl.BlockSpec((tm, d), lambda i: (i, 0)),
            pl.BlockSpec((tm, k), lambda i: (i, 0)),
            pl.BlockSpec((tm, k), lambda i: (i, 0)),
            pl.BlockSpec((k, d), lambda i: (0, 0)),
            pl.BlockSpec((k, d), lambda i: (0, 0)),
        ],
        out_specs=pl.BlockSpec((tm, d), lambda i: (i, 0)),
        out_shape=jax.ShapeDtypeStruct((m, d), F32),
        compiler_params=pltpu.CompilerParams(
            dimension_semantics=("arbitrary",),
            vmem_limit_bytes=_vmem_limit(est + (8 << 20))),
        name="oproj",
    )(x, a1, a2, w1, w2)


def _mlp_kernel(h_ref, g_ref, wup_ref, wdn_ref, fg_ref, o_ref, m_ref):
    f = pl.program_id(1)

    @pl.when(f == 0)
    def _():
        h = h_ref[...]
        m_ref[...] = _rms(h, g_ref[...]).astype(BF16)
        o_ref[...] = h

    a = jnp.dot(m_ref[...], wup_ref[...], preferred_element_type=F32)
    a = jnp.square(jnp.maximum(a, 0.0)).astype(BF16)
    o_ref[...] += jnp.dot(a, wdn_ref[...], preferred_element_type=F32)

    @pl.when(f == pl.num_programs(1) - 1)
    def _():
        o_ref[...] = _rms(o_ref[...], fg_ref[...])


def _mlp(h, g, wup, wdn, fg, tm, tf):
    m, d = h.shape
    ff = wup.shape[1]
    est = 2 * tm * d * 4 * 2 + tm * d * 2 + 2 * 2 * d * tf * 2 + tm * tf * 6 + tm * d * 4
    return pl.pallas_call(
        _mlp_kernel,
        grid=(m // tm, ff // tf),
        in_specs=[
            pl.BlockSpec((tm, d), lambda i, f: (i, 0)),
            pl.BlockSpec((1, d), lambda i, f: (0, 0)),
            pl.BlockSpec((d, tf), lambda i, f: (0, f)),
            pl.BlockSpec((tf, d), lambda i, f: (f, 0)),
            pl.BlockSpec((1, d), lambda i, f: (0, 0)),
        ],
        out_specs=pl.BlockSpec((tm, d), lambda i, f: (i, 0)),
        out_shape=jax.ShapeDtypeStruct((m, d), F32),
        scratch_shapes=[pltpu.VMEM((tm, d), BF16)],
        compiler_params=pltpu.CompilerParams(
            dimension_semantics=("arbitrary", "arbitrary"),
            vmem_limit_bytes=_vmem_limit(est + (8 << 20))),
        name="mlp",
    )(h, g, wup, wdn, fg)


def _rope_tables(pos, dim):
    inv = 1.0 / (ROPE_BASE ** (jnp.arange(0, dim, 2, dtype=F32) / dim))
    ang = pos[:, None] * inv[None, :]
    return jnp.cos(ang), jnp.sin(ang)


def kernel(x, meta_tokens, norm_mix_g, w_in, ret_out_g, q_norm_g, w_uq, kv_norm_g,
           w_uk, w_uv, mla_out_g, w_o, norm_mlp_g, w_up, w_down, final_norm_g):
    B, S, D = x.shape
    assert norm_mix_g.shape[0] == 1 and meta_tokens.shape == (N_META, D)
    M = B * S
    xm = x.reshape(M, D)

    w_in0 = w_in[0]
    w_in_aug = jnp.concatenate(
        [w_in0, w_in0[:, COL_KR:COL_KR + MLA_ROPE_DIM],
         jnp.zeros((D, PROJ_WIDTH - w_in0.shape[1] - MLA_ROPE_DIM), w_in0.dtype)], axis=1).astype(BF16)
    wq3 = w_uq[0].reshape(Q_LORA, MLA_HEADS, MLA_QK_DIM)
    wuq = jnp.concatenate(
        [wq3[:, :, :MLA_NOPE_DIM].reshape(Q_LORA, -1),
         wq3[:, :, MLA_NOPE_DIM:].reshape(Q_LORA, -1)], axis=1).astype(BF16)
    wukt = jnp.transpose(w_uk[0].reshape(KV_LORA, MLA_HEADS, MLA_NOPE_DIM), (1, 2, 0)).astype(BF16)
    wuv = jnp.transpose(w_uv[0].reshape(KV_LORA, MLA_HEADS, MLA_V_DIM), (1, 0, 2)).astype(BF16)
    wo1 = w_o[0, :RET_WIDTH].astype(BF16)
    wo2 = w_o[0, RET_WIDTH:].astype(BF16)
    wup = w_up[0].astype(BF16)
    wdn = w_down[0].astype(BF16)

    pos = jnp.arange(S, dtype=F32) + float(N_META)
    posm = jnp.arange(N_META, dtype=F32)

    def ret_tables(p):
        c, s = _rope_tables(p, RET_HEAD_DIM)
        return jnp.concatenate([c, c], 1), jnp.concatenate([-s, s], 1)

    def mla_tables(p):
        c, s = _rope_tables(p, MLA_ROPE_DIM)
        return jnp.concatenate([c, c, c, c], 1), jnp.concatenate([-s, s, -s, s], 1)

    cos2, sin2 = ret_tables(pos)
    cosm, sinm = ret_tables(posm)
    pc, ps = mla_tables(pos)
    pcm, psm = mla_tables(posm)

    g_mix = norm_mix_g[0].reshape(1, D)
    proj = _inproj(xm, g_mix, w_in_aug, tm=1024, tn=1280)
    proj_meta = _inproj(meta_tokens, g_mix, w_in_aug, tm=N_META, tn=1280)
    projv = proj.reshape(B, S, PROJ_WIDTH)

    ret = _retention(projv, proj_meta, cos2, sin2, cosm, sinm,
                     ret_out_g[0].reshape(1, RET_WIDTH), T=256)
    mla = _mla(projv, proj_meta, pc, ps, pcm, psm,
               q_norm_g[0].reshape(1, Q_LORA), kv_norm_g[0].reshape(1, KV_LORA),
               wuq, wukt, wuv, mla_out_g[0].reshape(1, MLA_WIDTH), TQ=256)

    h = _oproj(xm, ret.reshape(M, RET_WIDTH), mla.reshape(M, MLA_WIDTH), wo1, wo2, tm=512)
    out = _mlp(h, norm_mlp_g[0].reshape(1, D), wup, wdn, final_norm_g.reshape(1, D),
               tm=1024, tf=512)
    return out.reshape(B, S, D)
```

```python
import functools

import numpy as np
import jax
import jax.numpy as jnp
from jax import lax
from jax.experimental import pallas as pl
from jax.experimental.pallas import tpu as pltpu

F32 = jnp.float32
BF16 = jnp.bfloat16

CHUNK = 64
N_META = 16
RET_HEADS = 8
RET_HEAD_DIM = 128
MLA_HEADS = 8
MLA_V_DIM = 128
MLA_NOPE_DIM = 128
MLA_ROPE_DIM = 64
Q_LORA = 512
KV_LORA = 256
MLA_QK_DIM = MLA_NOPE_DIM + MLA_ROPE_DIM
ROPE_BASE = 10000.0
EPS = 1e-6
LOG2E = float(np.log2(np.e))

LANES = 128
RET_WIDTH = RET_HEADS * RET_HEAD_DIM
MLA_WIDTH = MLA_HEADS * MLA_V_DIM
COL_CQ = 4 * RET_WIDTH
COL_CKV = COL_CQ + Q_LORA
COL_KR = COL_CKV + KV_LORA
PROJ_WIDTH = 5120
KAUG = KV_LORA + LANES

VMEM_LIMIT_CAP = 60000 * 1024


def _vmem_limit(nbytes):
    return int(min(VMEM_LIMIT_CAP, nbytes))


def _rms(x, g):
    ms = jnp.mean(x * x, axis=-1, keepdims=True)
    return x * lax.rsqrt(ms + EPS) * g


def _inproj_kernel(x_ref, g_ref, w_ref, o_ref, u_ref):
    @pl.when(pl.program_id(1) == 0)
    def _():
        u_ref[...] = _rms(x_ref[...], g_ref[...]).astype(BF16)

    o_ref[...] = jnp.dot(u_ref[...], w_ref[...],
                         preferred_element_type=F32).astype(o_ref.dtype)


def _inproj(x, g, w, tm, tn):
    m, d = x.shape
    n = w.shape[1]
    est = 2 * tm * d * 4 + tm * d * 2 + 2 * d * tn * 2 + 2 * tm * tn * 2 + tm * tn * 4 + tm * d * 4
    return pl.pallas_call(
        _inproj_kernel,
        grid=(m // tm, n // tn),
        in_specs=[
            pl.BlockSpec((tm, d), lambda i, j: (i, 0)),
            pl.BlockSpec((1, d), lambda i, j: (0, 0)),
            pl.BlockSpec((d, tn), lambda i, j: (0, j)),
        ],
        out_specs=pl.BlockSpec((tm, tn), lambda i, j: (i, j)),
        out_shape=jax.ShapeDtypeStruct((m, n), BF16),
        scratch_shapes=[pltpu.VMEM((tm, d), BF16)],
        compiler_params=pltpu.CompilerParams(
            dimension_semantics=("arbitrary", "arbitrary"),
            vmem_limit_bytes=_vmem_limit(est + (8 << 20))),
        name="inproj",
    )(x, g, w)


def _rope128(x, cos2, sin2):
    return x * cos2 + pltpu.roll(x, 64, 1) * sin2


def _retention_kernel(q_ref, k_ref, v_ref, g_ref, cos_ref, sin_ref,
                      km_ref, vm_ref, cosm_ref, sinm_ref, gout_ref,
                      o_ref,
                      state_ref, d_ref, wq_ref, wk_ref, *, T):
    b = pl.program_id(0)
    t = pl.program_id(1)
    H = RET_HEADS
    dk = RET_HEAD_DIM

    def log_gamma(h):
        e = jnp.full((1, 1), -5.0 - h, F32)
        return jnp.log(1.0 - jnp.exp2(e))

    @pl.when((b == 0) & (t == 0))
    def _():
        ii = lax.broadcasted_iota(jnp.int32, (T, T), 0)
        jj = lax.broadcasted_iota(jnp.int32, (T, T), 1)
        dist = jnp.abs(ii - jj).astype(F32)
        vis = (jj // CHUNK) <= (ii // CHUNK)
        row = lax.broadcasted_iota(jnp.int32, (T, LANES), 0).astype(F32)
        for h in range(H):
            lg = log_gamma(h)
            d_ref[h] = jnp.where(vis, jnp.exp(lg * dist), 0.0)
            wq_ref[h] = jnp.exp(lg * (row + 1.0))
            wk_ref[h] = jnp.exp(lg * (T - 1.0 - row))

    @pl.when(t == 0)
    def _():
        rowm = lax.broadcasted_iota(jnp.int32, (N_META, LANES), 0).astype(F32)
        cm = cosm_ref[...]
        sm = sinm_ref[...]
        for h in range(H):
            lg = log_gamma(h)
            km = _rope128(km_ref[:, h * dk:(h + 1) * dk].astype(F32), cm, sm)
            km = km * jnp.exp(lg * (N_META - 1.0 - rowm))
            vm = vm_ref[:, h * dk:(h + 1) * dk]
            state_ref[h] = lax.dot_general(
                km.astype(BF16), vm, (((0,), (0,)), ((), ())),
                preferred_element_type=F32)

    cos2 = cos_ref[...]
    sin2 = sin_ref[...]
    scale = dk ** -0.5
    for h in range(H):
        cs = slice(h * dk, (h + 1) * dk)
        lg = log_gamma(h)
        q = _rope128(q_ref[0, :, cs].astype(F32), cos2, sin2) * scale
        k = _rope128(k_ref[0, :, cs].astype(F32), cos2, sin2)
        v = v_ref[0, :, cs]
        s = lax.dot_general(q.astype(BF16), k.astype(BF16),
                            (((1,), (1,)), ((), ())), preferred_element_type=F32)
        sd = (s * d_ref[h]).astype(BF16)
        st = state_ref[h]
        o = jnp.dot(sd, v, preferred_element_type=F32)
        o = o + jnp.dot((q * wq_ref[h]).astype(BF16), st.astype(BF16),
                        preferred_element_type=F32)
        kv = lax.dot_general((k * wk_ref[h]).astype(BF16), v,
                             (((0,), (0,)), ((), ())), preferred_element_type=F32)
        state_ref[h] = st * jnp.exp(lg * float(T)) + kv
        mu = jnp.mean(o, axis=-1, keepdims=True)
        oc = o - mu
        var = jnp.mean(oc * oc, axis=-1, keepdims=True)
        y = oc * lax.rsqrt(var + EPS) * gout_ref[:, cs]
        gate = g_ref[0, :, cs].astype(F32)
        gate = gate * (1.0 / (1.0 + jnp.exp(-gate)))
        o_ref[0, :, cs] = (gate * y).astype(o_ref.dtype)


def _retention(projv, proj_meta, cos2, sin2, cosm, sinm, gout, T):
    B, S, _ = projv.shape
    W = RET_WIDTH
    est = (2 * 4 * T * W * 2 + 2 * T * W * 2 + 4 * T * LANES * 4
           + RET_HEADS * (T * T + 2 * T * LANES + 128 * 128) * 4 + 16 * T * T * 4)
    kern = functools.partial(_retention_kernel, T=T)
    return pl.pallas_call(
        kern,
        grid=(B, S // T),
        in_specs=[
            pl.BlockSpec((1, T, W), lambda b, t: (b, t, 0)),
            pl.BlockSpec((1, T, W), lambda b, t: (b, t, 1)),
            pl.BlockSpec((1, T, W), lambda b, t: (b, t, 2)),
            pl.BlockSpec((1, T, W), lambda b, t: (b, t, 3)),
            pl.BlockSpec((T, LANES), lambda b, t: (t, 0)),
            pl.BlockSpec((T, LANES), lambda b, t: (t, 0)),
            pl.BlockSpec((N_META, W), lambda b, t: (0, 1)),
            pl.BlockSpec((N_META, W), lambda b, t: (0, 2)),
            pl.BlockSpec((N_META, LANES), lambda b, t: (0, 0)),
            pl.BlockSpec((N_META, LANES), lambda b, t: (0, 0)),
            pl.BlockSpec((1, W), lambda b, t: (0, 0)),
        ],
        out_specs=pl.BlockSpec((1, T, W), lambda b, t: (b, t, 0)),
        out_shape=jax.ShapeDtypeStruct((B, S, W), BF16),
        scratch_shapes=[
            pltpu.VMEM((RET_HEADS, RET_HEAD_DIM, RET_HEAD_DIM), F32),
            pltpu.VMEM((RET_HEADS, T, T), F32),
            pltpu.VMEM((RET_HEADS, T, LANES), F32),
            pltpu.VMEM((RET_HEADS, T, LANES), F32),
        ],
        compiler_params=pltpu.CompilerParams(
            dimension_semantics=("arbitrary", "arbitrary"),
            vmem_limit_bytes=_vmem_limit(est + (8 << 20))),
        name="retention",
    )(projv, projv, projv, projv, cos2, sin2, proj_meta, proj_meta, cosm, sinm, gout)


def _swap32(x):
    lane = lax.broadcasted_iota(jnp.int32, x.shape, 1)
    return jnp.where((lane % 64) < 32, pltpu.roll(x, 96, 1), pltpu.roll(x, 32, 1))


def _rope64x2(x, pc, ps):
    return x * pc + _swap32(x) * ps


def _mla_kernel(cq_ref, kv_ref, kvm_ref, pc_ref, ps_ref, pcm_ref, psm_ref,
                qg_ref, kvg_ref, wuq_ref, wukt_ref, wuv_ref, og_ref,
                o_ref,
                kaug_ref, qaug_ref, p_ref, m_ref, l_ref, a_ref, acc_ref,
                *, TQ, TK, G):
    i = pl.program_id(1)
    H = MLA_HEADS
    qscale = (MLA_QK_DIM ** -0.5) * LOG2E
    NEG = -1e30

    def prep_keys(kv, pc, ps):
        c = _rms(kv[:, :KV_LORA], kvg_ref[...])
        kr = _rope64x2(kv[:, KV_LORA:], pc, ps)
        return c.astype(BF16), kr.astype(BF16)

    @pl.when(i == 0)
    def _():
        kaug_ref[0:TK, :] = jnp.zeros((TK, KAUG), BF16)
        cm, krm = prep_keys(kvm_ref[...].astype(F32), pcm_ref[...], psm_ref[...])
        kaug_ref[0:N_META, 0:KV_LORA] = cm
        kaug_ref[0:N_META, KV_LORA:] = krm
        c, kr = prep_keys(kv_ref[0].astype(F32), pc_ref[...], ps_ref[...])
        kaug_ref[TK:, 0:KV_LORA] = c
        kaug_ref[TK:, KV_LORA:] = kr

    q0 = pl.multiple_of(i * TQ, TQ)
    cq = _rms(cq_ref[0].astype(F32), qg_ref[...]).astype(BF16)
    q = jnp.dot(cq, wuq_ref[...], preferred_element_type=F32)
    pcq = pc_ref[pl.ds(q0, TQ), :]
    psq = ps_ref[pl.ds(q0, TQ), :]
    lane = lax.broadcasted_iota(jnp.int32, (TQ, LANES), 1)
    for h in range(H):
        qn = q[:, h * MLA_NOPE_DIM:(h + 1) * MLA_NOPE_DIM].astype(BF16)
        ql = jnp.dot(qn, wukt_ref[h], preferred_element_type=F32) * qscale
        qaug_ref[h * TQ:(h + 1) * TQ, 0:KV_LORA] = ql.astype(BF16)
    for p in range(H // 2):
        c0 = H * MLA_NOPE_DIM + p * LANES
        rp = _rope64x2(q[:, c0:c0 + LANES], pcq, psq) * qscale
        qaug_ref[(2 * p) * TQ:(2 * p + 1) * TQ, KV_LORA:] = jnp.where(lane < 64, rp, 0.0).astype(BF16)
        qaug_ref[(2 * p + 1) * TQ:(2 * p + 2) * TQ, KV_LORA:] = jnp.where(lane >= 64, rp, 0.0).astype(BF16)

    def block(kb, bias, first):
        W = kb.shape[0]
        vb = kb[:, 0:KV_LORA]
        HG = H // G
        AHEAD = 2
        def scores(g):
            gs = slice(g * HG * TQ, (g + 1) * HG * TQ)
            return lax.dot_general(qaug_ref[gs, :], kb, (((1,), (1,)), ((), ())),
                                   preferred_element_type=F32)

        s_groups = {g: scores(g) for g in range(min(AHEAD, G))}
        for g in range(G):
            gs = slice(g * HG * TQ, (g + 1) * HG * TQ)
            s_g = s_groups.pop(g)
            for hh in range(HG):
                h = g * HG + hh
                rs = slice(h * TQ, (h + 1) * TQ)
                s = s_g[hh * TQ:(hh + 1) * TQ]
                if bias is not None:
                    s = s + bias
                cols = [s[:, c * LANES:(c + 1) * LANES] for c in range(W // LANES)]
                m_cur = jnp.max(functools.reduce(jnp.maximum, cols), axis=-1, keepdims=True)
                if first:
                    m_new = jnp.broadcast_to(m_cur, (TQ, LANES))
                else:
                    m_prev = m_ref[rs]
                    m_new = jnp.maximum(m_prev, m_cur)
                    alpha = jnp.exp2(m_prev - m_new)
                    a_ref[rs] = alpha
                ps = [jnp.exp2(c - m_new) for c in cols]
                psum = functools.reduce(lambda x, y: x + y, ps)
                if first:
                    l_ref[rs] = psum
                else:
                    l_ref[rs] = alpha * l_ref[rs] + psum
                m_ref[rs] = m_new
                for c, pc_ in enumerate(ps):
                    p_ref[rs, c * LANES:(c + 1) * LANES] = pc_.astype(BF16)
            o = jnp.dot(p_ref[gs, 0:W], vb, preferred_element_type=F32)
            if first:
                acc_ref[gs, :] = o
            else:
                a = a_ref[gs, :]
                acc_ref[gs, 0:LANES] = a * acc_ref[gs, 0:LANES] + o[:, 0:LANES]
                acc_ref[gs, LANES:] = a * acc_ref[gs, LANES:] + o[:, LANES:]
            if g + AHEAD < G:
                s_groups[g + AHEAD] = scores(g + AHEAD)

    colh = lax.broadcasted_iota(jnp.int32, (TQ, TK), 1)
    bias_h = jnp.where(colh < N_META, 0.0, NEG).astype(F32)
    rq = lax.broadcasted_iota(jnp.int32, (TQ, TK), 0) // CHUNK
    bias_d = jnp.where(colh // CHUNK <= rq, 0.0, NEG).astype(F32)

    diag0 = pl.multiple_of((i + 1) * TK, TK)
    kb_hd = jnp.concatenate([kaug_ref[0:TK, :], kaug_ref[pl.ds(diag0, TK), :]], axis=0)
    block(kb_hd, jnp.concatenate([bias_h, bias_d], axis=1), True)

    def body(t, carry):
        k0 = pl.multiple_of(TK + 2 * TK * t, TK)
        block(kaug_ref[pl.ds(k0, 2 * TK), :], None, False)
        return carry

    lax.fori_loop(0, i // 2, body, 0)

    @pl.when(i % 2 == 1)
    def _():
        block(kaug_ref[pl.ds(pl.multiple_of(i * TK, TK), TK), :], None, False)

    outs = []
    for h in range(H):
        rs = slice(h * TQ, (h + 1) * TQ)
        inv_l = 1.0 / jnp.sum(l_ref[rs], axis=-1, keepdims=True)
        o = (acc_ref[rs] * inv_l).astype(BF16)
        outs.append(jnp.dot(o, wuv_ref[h], preferred_element_type=F32))
    mla = jnp.concatenate(outs, axis=-1)
    o_ref[0] = _rms(mla, og_ref[...]).astype(o_ref.dtype)


def _mla(projv, proj_meta, pc, ps, pcm, psm, qg, kvg, wuq, wukt, wuv, og, TQ, G):
    B, S, _ = projv.shape
    TK = TQ
    R = MLA_HEADS * TQ
    nq = wuq.shape[1]
    est = (2 * TQ * Q_LORA * 2 + 2 * S * KAUG * 2 + 4 * S * LANES * 4
           + 2 * (Q_LORA * nq + 2 * MLA_HEADS * 128 * KV_LORA) * 2 + 2 * TQ * MLA_WIDTH * 2
           + (TK + S) * KAUG * 2 + R * KAUG * 2 + R * 2 * TK * 2 + 3 * R * LANES * 4 + R * KV_LORA * 4
           + 3 * R * TK * 4 + S * KAUG * 4 * 2)
    kern = functools.partial(_mla_kernel, TQ=TQ, TK=TK, G=G)
    const2 = lambda b, i: (0, 0)
    const3 = lambda b, i: (0, 0, 0)
    return pl.pallas_call(
        kern,
        grid=(B, S // TQ),
        in_specs=[
            pl.BlockSpec((1, TQ, Q_LORA), lambda b, i: (b, i, COL_CQ // Q_LORA)),
            pl.BlockSpec((1, S, KAUG), lambda b, i: (b, 0, COL_CKV // KAUG)),
            pl.BlockSpec((N_META, KAUG), lambda b, i: (0, COL_CKV // KAUG)),
            pl.BlockSpec((S, LANES), const2),
            pl.BlockSpec((S, LANES), const2),
            pl.BlockSpec((N_META, LANES), const2),
            pl.BlockSpec((N_META, LANES), const2),
            pl.BlockSpec((1, Q_LORA), const2),
            pl.BlockSpec((1, KV_LORA), const2),
            pl.BlockSpec((Q_LORA, nq), const2),
            pl.BlockSpec((MLA_HEADS, MLA_NOPE_DIM, KV_LORA), const3),
            pl.BlockSpec((MLA_HEADS, KV_LORA, MLA_V_DIM), const3),
            pl.BlockSpec((1, MLA_WIDTH), const2),
        ],
        out_specs=pl.BlockSpec((1, TQ, MLA_WIDTH), lambda b, i: (b, i, 0)),
        out_shape=jax.ShapeDtypeStruct((B, S, MLA_WIDTH), BF16),
        scratch_shapes=[
            pltpu.VMEM((TK + S, KAUG), BF16),
            pltpu.VMEM((R, KAUG), BF16),
            pltpu.VMEM((R, 2 * TK), BF16),
            pltpu.VMEM((R, LANES), F32),
            pltpu.VMEM((R, LANES), F32),
            pltpu.VMEM((R, LANES), F32),
            pltpu.VMEM((R, KV_LORA), F32),
        ],
        compiler_params=pltpu.CompilerParams(
            dimension_semantics=("arbitrary", "arbitrary"),
            vmem_limit_bytes=_vmem_limit(est + (8 << 20))),
        name="mla",
    )(projv, projv, proj_meta, pc, ps, pcm, psm, qg, kvg, wuq, wukt, wuv, og)


def _oproj_kernel(x_ref, a1_ref, a2_ref, w1_ref, w2_ref, o_ref):
    acc = jnp.dot(a1_ref[...], w1_ref[...], preferred_element_type=F32)
    acc = acc + jnp.dot(a2_ref[...], w2_ref[...], preferred_element_type=F32)
    o_ref[...] = x_ref[...] + acc


def _oproj(x, a1, a2, w1, w2, tm):
    m, d = x.shape
    k = a1.shape[1]
    est = 2 * tm * d * 4 * 2 + 2 * 2 * tm * k * 2 + 2 * 2 * k * d * 2 + tm * d * 4
    return pl.pallas_call(
        _oproj_kernel,
        grid=(m // tm,),
        in_specs=[
            pl.BlockSpec((tm, d), lambda i: (i, 0)),
            pl.BlockSpec((tm, k), lambda i: (i, 0)),
            pl.BlockSpec((tm, k), lambda i: (i, 0)),
            pl.BlockSpec((k, d), lambda i: (0, 0)),
            pl.BlockSpec((k, d), lambda i: (0, 0)),
        ],
        out_specs=pl.BlockSpec((tm, d), lambda i: (i, 0)),
        out_shape=jax.ShapeDtypeStruct((m, d), F32),
        compiler_params=pltpu.CompilerParams(
            dimension_semantics=("arbitrary",),
            vmem_limit_bytes=_vmem_limit(est + (8 << 20))),
        name="oproj",
    )(x, a1, a2, w1, w2)


def _mlp_kernel(h_ref, g_ref, wup_ref, wdn_ref, fg_ref, o_ref, m_ref):
    f = pl.program_id(1)

    @pl.when(f == 0)
    def _():
        h = h_ref[...]
        m_ref[...] = _rms(h, g_ref[...]).astype(BF16)
        o_ref[...] = h

    a = jnp.dot(m_ref[...], wup_ref[...], preferred_element_type=F32)
    a = jnp.square(jnp.maximum(a, 0.0)).astype(BF16)
    o_ref[...] += jnp.dot(a, wdn_ref[...], preferred_element_type=F32)

    @pl.when(f == pl.num_programs(1) - 1)
    def _():
        o_ref[...] = _rms(o_ref[...], fg_ref[...])


def _mlp(h, g, wup, wdn, fg, tm, tf):
    m, d = h.shape
    ff = wup.shape[1]
    est = 2 * tm * d * 4 * 2 + tm * d * 2 + 2 * 2 * d * tf * 2 + tm * tf * 6 + tm * d * 4
    return pl.pallas_call(
        _mlp_kernel,
        grid=(m // tm, ff // tf),
        in_specs=[
            pl.BlockSpec((tm, d), lambda i, f: (i, 0)),
            pl.BlockSpec((1, d), lambda i, f: (0, 0)),
            pl.BlockSpec((d, tf), lambda i, f: (0, f)),
            pl.BlockSpec((tf, d), lambda i, f: (f, 0)),
            pl.BlockSpec((1, d), lambda i, f: (0, 0)),
        ],
        out_specs=pl.BlockSpec((tm, d), lambda i, f: (i, 0)),
        out_shape=jax.ShapeDtypeStruct((m, d), F32),
        scratch_shapes=[pltpu.VMEM((tm, d), BF16)],
        compiler_params=pltpu.CompilerParams(
            dimension_semantics=("arbitrary", "arbitrary"),
            vmem_limit_bytes=_vmem_limit(est + (8 << 20))),
        name="mlp",
    )(h, g, wup, wdn, fg)


def _rope_tables(pos, dim):
    inv = 1.0 / (ROPE_BASE ** (jnp.arange(0, dim, 2, dtype=F32) / dim))
    ang = pos[:, None] * inv[None, :]
    return jnp.cos(ang), jnp.sin(ang)


def kernel(x, meta_tokens, norm_mix_g, w_in, ret_out_g, q_norm_g, w_uq, kv_norm_g,
           w_uk, w_uv, mla_out_g, w_o, norm_mlp_g, w_up, w_down, final_norm_g):
    B, S, D = x.shape
    assert norm_mix_g.shape[0] == 1 and meta_tokens.shape == (N_META, D)
    M = B * S
    xm = x.reshape(M, D)

    w_in0 = w_in[0]
    w_in_aug = jnp.concatenate(
        [w_in0, w_in0[:, COL_KR:COL_KR + MLA_ROPE_DIM],
         jnp.zeros((D, PROJ_WIDTH - w_in0.shape[1] - MLA_ROPE_DIM), w_in0.dtype)], axis=1).astype(BF16)
    wq3 = w_uq[0].reshape(Q_LORA, MLA_HEADS, MLA_QK_DIM)
    wuq = jnp.concatenate(
        [wq3[:, :, :MLA_NOPE_DIM].reshape(Q_LORA, -1),
         wq3[:, :, MLA_NOPE_DIM:].reshape(Q_LORA, -1)], axis=1).astype(BF16)
    wukt = jnp.transpose(w_uk[0].reshape(KV_LORA, MLA_HEADS, MLA_NOPE_DIM), (1, 2, 0)).astype(BF16)
    wuv = jnp.transpose(w_uv[0].reshape(KV_LORA, MLA_HEADS, MLA_V_DIM), (1, 0, 2)).astype(BF16)
    wo1 = w_o[0, :RET_WIDTH].astype(BF16)
    wo2 = w_o[0, RET_WIDTH:].astype(BF16)
    wup = w_up[0].astype(BF16)
    wdn = w_down[0].astype(BF16)

    pos = jnp.arange(S, dtype=F32) + float(N_META)
    posm = jnp.arange(N_META, dtype=F32)

    def ret_tables(p):
        c, s = _rope_tables(p, RET_HEAD_DIM)
        return jnp.concatenate([c, c], 1), jnp.concatenate([-s, s], 1)

    def mla_tables(p):
        c, s = _rope_tables(p, MLA_ROPE_DIM)
        return jnp.concatenate([c, c, c, c], 1), jnp.concatenate([-s, s, -s, s], 1)

    cos2, sin2 = ret_tables(pos)
    cosm, sinm = ret_tables(posm)
    pc, ps = mla_tables(pos)
    pcm, psm = mla_tables(posm)

    g_mix = norm_mix_g[0].reshape(1, D)
    proj = _inproj(xm, g_mix, w_in_aug, tm=1024, tn=1280)
    proj_meta = _inproj(meta_tokens, g_mix, w_in_aug, tm=N_META, tn=1280)
    projv = proj.reshape(B, S, PROJ_WIDTH)

    ret = _retention(projv, proj_meta, cos2, sin2, cosm, sinm,
                     ret_out_g[0].reshape(1, RET_WIDTH), T=256)
    mla = _mla(projv, proj_meta, pc, ps, pcm, psm,
               q_norm_g[0].reshape(1, Q_LORA), kv_norm_g[0].reshape(1, KV_LORA),
               wuq, wukt, wuv, mla_out_g[0].reshape(1, MLA_WIDTH), TQ=256, G=4)

    h = _oproj(xm, ret.reshape(M, RET_WIDTH), mla.reshape(M, MLA_WIDTH), wo1, wo2, tm=512)
    out = _mlp(h, norm_mlp_g[0].reshape(1, D), wup, wdn, final_norm_g.reshape(1, D),
               tm=1024, tf=512)
    return out.reshape(B, S, D)
```

```python
import functools

import numpy as np
import jax
import jax.numpy as jnp
from jax import lax
from jax.experimental import pallas as pl
from jax.experimental.pallas import tpu as pltpu

F32 = jnp.float32
BF16 = jnp.bfloat16

CHUNK = 64
N_META = 16
RET_HEADS = 8
RET_HEAD_DIM = 128
MLA_HEADS = 8
MLA_V_DIM = 128
MLA_NOPE_DIM = 128
MLA_ROPE_DIM = 64
Q_LORA = 512
KV_LORA = 256
MLA_QK_DIM = MLA_NOPE_DIM + MLA_ROPE_DIM
ROPE_BASE = 10000.0
EPS = 1e-6
LOG2E = float(np.log2(np.e))

LANES = 128
RET_WIDTH = RET_HEADS * RET_HEAD_DIM
MLA_WIDTH = MLA_HEADS * MLA_V_DIM
COL_CQ = 4 * RET_WIDTH
COL_CKV = COL_CQ + Q_LORA
COL_KR = COL_CKV + KV_LORA
PROJ_WIDTH = 5120
KAUG = KV_LORA + LANES
HEAD_PAD = 256

VMEM_LIMIT_CAP = 60000 * 1024


def _vmem_limit(nbytes):
    return int(min(VMEM_LIMIT_CAP, nbytes))


def _rms(x, g):
    ms = jnp.mean(x * x, axis=-1, keepdims=True)
    return x * lax.rsqrt(ms + EPS) * g


def _inproj_kernel(x_ref, g_ref, w_ref, o_ref, u_ref):
    @pl.when(pl.program_id(1) == 0)
    def _():
        u_ref[...] = _rms(x_ref[...], g_ref[...]).astype(BF16)

    o_ref[...] = jnp.dot(u_ref[...], w_ref[...],
                         preferred_element_type=F32).astype(o_ref.dtype)


def _inproj(x, g, w, tm, tn):
    m, d = x.shape
    n = w.shape[1]
    est = 2 * tm * d * 4 + tm * d * 2 + 2 * d * tn * 2 + 2 * tm * tn * 2 + tm * tn * 4 + tm * d * 4
    return pl.pallas_call(
        _inproj_kernel,
        grid=(m // tm, n // tn),
        in_specs=[
            pl.BlockSpec((tm, d), lambda i, j: (i, 0)),
            pl.BlockSpec((1, d), lambda i, j: (0, 0)),
            pl.BlockSpec((d, tn), lambda i, j: (0, j)),
        ],
        out_specs=pl.BlockSpec((tm, tn), lambda i, j: (i, j)),
        out_shape=jax.ShapeDtypeStruct((m, n), BF16),
        scratch_shapes=[pltpu.VMEM((tm, d), BF16)],
        compiler_params=pltpu.CompilerParams(
            dimension_semantics=("arbitrary", "arbitrary"),
            vmem_limit_bytes=_vmem_limit(est + (8 << 20))),
        name="inproj",
    )(x, g, w)


def _rope128(x, cos2, sin2):
    return x * cos2 + pltpu.roll(x, 64, 1) * sin2


def _retention_kernel(q_ref, k_ref, v_ref, g_ref, cos_ref, sin_ref,
                      km_ref, vm_ref, cosm_ref, sinm_ref, gout_ref,
                      o_ref,
                      state_ref, d_ref, wq_ref, wk_ref, *, T):
    b = pl.program_id(0)
    t = pl.program_id(1)
    H = RET_HEADS
    dk = RET_HEAD_DIM

    def log_gamma(h):
        e = jnp.full((1, 1), -5.0 - h, F32)
        return jnp.log(1.0 - jnp.exp2(e))

    @pl.when((b == 0) & (t == 0))
    def _():
        ii = lax.broadcasted_iota(jnp.int32, (T, T), 0)
        jj = lax.broadcasted_iota(jnp.int32, (T, T), 1)
        dist = jnp.abs(ii - jj).astype(F32)
        vis = (jj // CHUNK) <= (ii // CHUNK)
        row = lax.broadcasted_iota(jnp.int32, (T, LANES), 0).astype(F32)
        for h in range(H):
            lg = log_gamma(h)
            d_ref[h] = jnp.where(vis, jnp.exp(lg * dist), 0.0)
            wq_ref[h] = jnp.exp(lg * (row + 1.0))
            wk_ref[h] = jnp.exp(lg * (T - 1.0 - row))

    @pl.when(t == 0)
    def _():
        rowm = lax.broadcasted_iota(jnp.int32, (N_META, LANES), 0).astype(F32)
        cm = cosm_ref[...]
        sm = sinm_ref[...]
        for h in range(H):
            lg = log_gamma(h)
            km = _rope128(km_ref[:, h * dk:(h + 1) * dk].astype(F32), cm, sm)
            km = km * jnp.exp(lg * (N_META - 1.0 - rowm))
            vm = vm_ref[:, h * dk:(h + 1) * dk]
            state_ref[h] = lax.dot_general(
                km.astype(BF16), vm, (((0,), (0,)), ((), ())),
                preferred_element_type=F32)

    cos2 = cos_ref[...]
    sin2 = sin_ref[...]
    scale = dk ** -0.5
    pre = []
    for h in range(H):
        cs = slice(h * dk, (h + 1) * dk)
        lg = log_gamma(h)
        q = _rope128(q_ref[0, :, cs].astype(F32), cos2, sin2) * scale
        k = _rope128(k_ref[0, :, cs].astype(F32), cos2, sin2)
        v = v_ref[0, :, cs]
        s = lax.dot_general(q.astype(BF16), k.astype(BF16),
                            (((1,), (1,)), ((), ())), preferred_element_type=F32)
        st = state_ref[h]
        inter = jnp.dot((q * wq_ref[h]).astype(BF16), st.astype(BF16),
                        preferred_element_type=F32)
        kv = lax.dot_general((k * wk_ref[h]).astype(BF16), v,
                             (((0,), (0,)), ((), ())), preferred_element_type=F32)
        state_ref[h] = st * jnp.exp(lg * float(T)) + kv
        pre.append((s, inter))
    for h in range(H):
        cs = slice(h * dk, (h + 1) * dk)
        s, inter = pre[h]
        sd = (s * d_ref[h]).astype(BF16)
        o = jnp.dot(sd, v_ref[0, :, cs], preferred_element_type=F32) + inter
        mu = jnp.mean(o, axis=-1, keepdims=True)
        oc = o - mu
        var = jnp.mean(oc * oc, axis=-1, keepdims=True)
        y = oc * lax.rsqrt(var + EPS) * gout_ref[:, cs]
        gate = g_ref[0, :, cs].astype(F32)
        gate = gate * (1.0 / (1.0 + jnp.exp(-gate)))
        o_ref[0, :, cs] = (gate * y).astype(o_ref.dtype)


def _retention(projv, proj_meta, cos2, sin2, cosm, sinm, gout, T):
    B, S, _ = projv.shape
    W = RET_WIDTH
    est = (2 * 4 * T * W * 2 + 2 * T * W * 2 + 4 * T * LANES * 4
           + RET_HEADS * (T * T + 2 * T * LANES + 128 * 128) * 4 + 16 * T * T * 4)
    kern = functools.partial(_retention_kernel, T=T)
    return pl.pallas_call(
        kern,
        grid=(B, S // T),
        in_specs=[
            pl.BlockSpec((1, T, W), lambda b, t: (b, t, 0)),
            pl.BlockSpec((1, T, W), lambda b, t: (b, t, 1)),
            pl.BlockSpec((1, T, W), lambda b, t: (b, t, 2)),
            pl.BlockSpec((1, T, W), lambda b, t: (b, t, 3)),
            pl.BlockSpec((T, LANES), lambda b, t: (t, 0)),
            pl.BlockSpec((T, LANES), lambda b, t: (t, 0)),
            pl.BlockSpec((N_META, W), lambda b, t: (0, 1)),
            pl.BlockSpec((N_META, W), lambda b, t: (0, 2)),
            pl.BlockSpec((N_META, LANES), lambda b, t: (0, 0)),
            pl.BlockSpec((N_META, LANES), lambda b, t: (0, 0)),
            pl.BlockSpec((1, W), lambda b, t: (0, 0)),
        ],
        out_specs=pl.BlockSpec((1, T, W), lambda b, t: (b, t, 0)),
        out_shape=jax.ShapeDtypeStruct((B, S, W), BF16),
        scratch_shapes=[
            pltpu.VMEM((RET_HEADS, RET_HEAD_DIM, RET_HEAD_DIM), F32),
            pltpu.VMEM((RET_HEADS, T, T), F32),
            pltpu.VMEM((RET_HEADS, T, LANES), F32),
            pltpu.VMEM((RET_HEADS, T, LANES), F32),
        ],
        compiler_params=pltpu.CompilerParams(
            dimension_semantics=("arbitrary", "arbitrary"),
            vmem_limit_bytes=_vmem_limit(est + (8 << 20))),
        name="retention",
    )(projv, projv, projv, projv, cos2, sin2, proj_meta, proj_meta, cosm, sinm, gout)


def _swap32(x):
    lane = lax.broadcasted_iota(jnp.int32, x.shape, 1)
    return jnp.where((lane % 64) < 32, pltpu.roll(x, 96, 1), pltpu.roll(x, 32, 1))


def _rope64x2(x, pc, ps):
    return x * pc + _swap32(x) * ps


def _mla_kernel(cq_ref, kv_ref, kvm_ref, pc_ref, ps_ref, pcm_ref, psm_ref,
                qg_ref, kvg_ref, wuq_ref, wukt_ref, sel_ref, wuv_ref, og_ref,
                o_ref,
                kaug_ref, kt_ref, qh_ref, p_ref, m_ref, l_ref, a_ref, acc_ref,
                *, TQ, TK, G):
    i = pl.program_id(1)
    H = MLA_HEADS
    NU = kt_ref.shape[0]
    qscale = (MLA_QK_DIM ** -0.5) * LOG2E
    NEG = -1e30

    def prep_keys(kv, pc, ps):
        c = _rms(kv[:, :KV_LORA], kvg_ref[...])
        kr = _rope64x2(kv[:, KV_LORA:], pc, ps)
        return c.astype(BF16), kr.astype(BF16)

    @pl.when(i == 0)
    def _():
        kaug_ref[0:TK, :] = jnp.zeros((TK, KAUG), BF16)
        cm, krm = prep_keys(kvm_ref[...].astype(F32), pcm_ref[...], psm_ref[...])
        kaug_ref[0:N_META, 0:KV_LORA] = cm
        kaug_ref[0:N_META, KV_LORA:] = krm
        c, kr = prep_keys(kv_ref[0].astype(F32), pc_ref[...], ps_ref[...])
        kaug_ref[TK:, 0:KV_LORA] = c
        kaug_ref[TK:, KV_LORA:] = kr
        nt = (((1,), (1,)), ((), ()))
        for u in range(NU):
            ku = kaug_ref[u * TK:(u + 1) * TK, :]
            kn = lax.dot_general(wukt_ref[...], ku[:, 0:KV_LORA], nt,
                                 preferred_element_type=F32).astype(BF16)
            krt = lax.dot_general(sel_ref[...], ku[:, KV_LORA:], nt,
                                  preferred_element_type=F32).astype(BF16)
            for h in range(H):
                kt_ref[u, h * HEAD_PAD:h * HEAD_PAD + LANES, :] = kn[h * LANES:(h + 1) * LANES]
                kt_ref[u, h * HEAD_PAD + LANES:(h + 1) * HEAD_PAD, :] = krt

    q0 = pl.multiple_of(i * TQ, TQ)
    cq = _rms(cq_ref[0].astype(F32), qg_ref[...]).astype(BF16)
    q = jnp.dot(cq, wuq_ref[...], preferred_element_type=F32)
    pcq = pc_ref[pl.ds(q0, TQ), :]
    psq = ps_ref[pl.ds(q0, TQ), :]
    for h in range(H):
        c0 = h * HEAD_PAD
        qh_ref[h * TQ:(h + 1) * TQ, 0:LANES] = (q[:, c0:c0 + LANES] * qscale).astype(BF16)
        rp = _rope64x2(q[:, c0 + LANES:c0 + HEAD_PAD], pcq, psq) * qscale
        qh_ref[h * TQ:(h + 1) * TQ, LANES:] = rp.astype(BF16)

    def block(units, bias, first):
        W = TK * len(units)
        vb = jnp.concatenate(
            [kaug_ref[pl.ds(pl.multiple_of(u * TK, TK), TK), 0:KV_LORA] for u in units], axis=0)
        HG = H // G
        AHEAD = 2

        def scores(g):
            out = []
            for h in range(g * HG, (g + 1) * HG):
                qh = qh_ref[h * TQ:(h + 1) * TQ, :]
                out.append([jnp.dot(qh, kt_ref[u, h * HEAD_PAD:(h + 1) * HEAD_PAD, :],
                                    preferred_element_type=F32) for u in units])
            return out

        s_groups = {g: scores(g) for g in range(min(AHEAD, G))}
        for g in range(G):
            gs = slice(g * HG * TQ, (g + 1) * HG * TQ)
            s_g = s_groups.pop(g)
            for hh in range(HG):
                h = g * HG + hh
                rs = slice(h * TQ, (h + 1) * TQ)
                cols = []
                for ui, s in enumerate(s_g[hh]):
                    if bias is not None and bias[ui] is not None:
                        s = s + bias[ui]
                    cols += [s[:, c * LANES:(c + 1) * LANES] for c in range(TK // LANES)]
                m_cur = jnp.max(functools.reduce(jnp.maximum, cols), axis=-1, keepdims=True)
                if first:
                    m_new = jnp.broadcast_to(m_cur, (TQ, LANES))
                else:
                    m_prev = m_ref[rs]
                    m_new = jnp.maximum(m_prev, m_cur)
                    alpha = jnp.exp2(m_prev - m_new)
                    a_ref[rs] = alpha
                ps = [jnp.exp2(c - m_new) for c in cols]
                psum = functools.reduce(lambda x, y: x + y, ps)
                if first:
                    l_ref[rs] = psum
                else:
                    l_ref[rs] = alpha * l_ref[rs] + psum
                m_ref[rs] = m_new
                for c, pc_ in enumerate(ps):
                    p_ref[rs, c * LANES:(c + 1) * LANES] = pc_.astype(BF16)
            o = jnp.dot(p_ref[gs, 0:W], vb, preferred_element_type=F32)
            if first:
                acc_ref[gs, :] = o
            else:
                a = a_ref[gs, :]
                acc_ref[gs, 0:LANES] = a * acc_ref[gs, 0:LANES] + o[:, 0:LANES]
                acc_ref[gs, LANES:] = a * acc_ref[gs, LANES:] + o[:, LANES:]
            if g + AHEAD < G:
                s_groups[g + AHEAD] = scores(g + AHEAD)

    colh = lax.broadcasted_iota(jnp.int32, (TQ, TK), 1)
    bias_h = jnp.where(colh < N_META, 0.0, NEG).astype(F32)
    rq = lax.broadcasted_iota(jnp.int32, (TQ, TK), 0) // CHUNK
    bias_d = jnp.where(colh // CHUNK <= rq, 0.0, NEG).astype(F32)

    block([0, i + 1], [bias_h, bias_d], True)

    def body(t, carry):
        block([2 * t + 1, 2 * t + 2], None, False)
        return carry

    lax.fori_loop(0, i // 2, body, 0)

    @pl.when(i % 2 == 1)
    def _():
        block([i], None, False)

    outs = []
    for h in range(H):
        rs = slice(h * TQ, (h + 1) * TQ)
        inv_l = 1.0 / jnp.sum(l_ref[rs], axis=-1, keepdims=True)
        o = (acc_ref[rs] * inv_l).astype(BF16)
        outs.append(jnp.dot(o, wuv_ref[h], preferred_element_type=F32))
    mla = jnp.concatenate(outs, axis=-1)
    o_ref[0] = _rms(mla, og_ref[...]).astype(o_ref.dtype)


def _mla(projv, proj_meta, pc, ps, pcm, psm, qg, kvg, wuq, wukt, sel, wuv, og, TQ, G):
    B, S, _ = projv.shape
    TK = TQ
    NU = 1 + S // TK
    R = MLA_HEADS * TQ
    nq = wuq.shape[1]
    HP = MLA_HEADS * HEAD_PAD
    est = (2 * TQ * Q_LORA * 2 + 2 * S * KAUG * 2 + 4 * S * LANES * 4
           + 2 * (Q_LORA * nq + 2 * MLA_HEADS * 128 * KV_LORA + LANES * LANES) * 2 + 2 * TQ * MLA_WIDTH * 2
           + NU * TK * KAUG * 2 + NU * HP * TK * 2 + R * HEAD_PAD * 2 + R * 2 * TK * 2
           + 3 * R * LANES * 4 + R * KV_LORA * 4
           + 3 * R * TK * 4 + S * KAUG * 4 * 2)
    kern = functools.partial(_mla_kernel, TQ=TQ, TK=TK, G=G)
    const2 = lambda b, i: (0, 0)
    const3 = lambda b, i: (0, 0, 0)
    return pl.pallas_call(
        kern,
        grid=(B, S // TQ),
        in_specs=[
            pl.BlockSpec((1, TQ, Q_LORA), lambda b, i: (b, i, COL_CQ // Q_LORA)),
            pl.BlockSpec((1, S, KAUG), lambda b, i: (b, 0, COL_CKV // KAUG)),
            pl.BlockSpec((N_META, KAUG), lambda b, i: (0, COL_CKV // KAUG)),
            pl.BlockSpec((S, LANES), const2),
            pl.BlockSpec((S, LANES), const2),
            pl.BlockSpec((N_META, LANES), const2),
            pl.BlockSpec((N_META, LANES), const2),
            pl.BlockSpec((1, Q_LORA), const2),
            pl.BlockSpec((1, KV_LORA), const2),
            pl.BlockSpec((Q_LORA, nq), const2),
            pl.BlockSpec((MLA_HEADS * MLA_NOPE_DIM, KV_LORA), const2),
            pl.BlockSpec((LANES, LANES), const2),
            pl.BlockSpec((MLA_HEADS, KV_LORA, MLA_V_DIM), const3),
            pl.BlockSpec((1, MLA_WIDTH), const2),
        ],
        out_specs=pl.BlockSpec((1, TQ, MLA_WIDTH), lambda b, i: (b, i, 0)),
        out_shape=jax.ShapeDtypeStruct((B, S, MLA_WIDTH), BF16),
        scratch_shapes=[
            pltpu.VMEM((NU * TK, KAUG), BF16),
            pltpu.VMEM((NU, HP, TK), BF16),
            pltpu.VMEM((R, HEAD_PAD), BF16),
            pltpu.VMEM((R, 2 * TK), BF16),
            pltpu.VMEM((R, LANES), F32),
            pltpu.VMEM((R, LANES), F32),
            pltpu.VMEM((R, LANES), F32),
            pltpu.VMEM((R, KV_LORA), F32),
        ],
        compiler_params=pltpu.CompilerParams(
            dimension_semantics=("arbitrary", "arbitrary"),
            vmem_limit_bytes=_vmem_limit(est + (8 << 20))),
        name="mla",
    )(projv, projv, proj_meta, pc, ps, pcm, psm, qg, kvg, wuq, wukt, sel, wuv, og)


def _oproj_kernel(x_ref, a1_ref, a2_ref, w1_ref, w2_ref, o_ref):
    acc = jnp.dot(a1_ref[...], w1_ref[...], preferred_element_type=F32)
    acc = acc + jnp.dot(a2_ref[...], w2_ref[...], preferred_element_type=F32)
    o_ref[...] = x_ref[...] + acc


def _oproj(x, a1, a2, w1, w2, tm):
    m, d = x.shape
    k = a1.shape[1]
    est = 2 * tm * d * 4 * 2 + 2 * 2 * tm * k * 2 + 2 * 2 * k * d * 2 + tm * d * 4
    return pl.pallas_call(
        _oproj_kernel,
        grid=(m // tm,),
        in_specs=[
            pl.BlockSpec((tm, d), lambda i: (i, 0)),
            pl.BlockSpec((tm, k), lambda i: (i, 0)),
            pl.BlockSpec((tm, k), lambda i: (i, 0)),
            pl.BlockSpec((k, d), lambda i: (0, 0)),
            pl.BlockSpec((k, d), lambda i: (0, 0)),
        ],
        out_specs=pl.BlockSpec((tm, d), lambda i: (i, 0)),
        out_shape=jax.ShapeDtypeStruct((m, d), F32),
        compiler_params=pltpu.CompilerParams(
            dimension_semantics=("arbitrary",),
            vmem_limit_bytes=_vmem_limit(est + (8 << 20))),
        name="oproj",
    )(x, a1, a2, w1, w2)


def _mlp_kernel(h_ref, g_ref, wup_ref, wdn_ref, fg_ref, o_ref, m_ref):
    f = pl.program_id(1)

    @pl.when(f == 0)
    def _():
        h = h_ref[...]
        m_ref[...] = _rms(h, g_ref[...]).astype(BF16)
        o_ref[...] = h

    a = jnp.dot(m_ref[...], wup_ref[...], preferred_element_type=F32)
    a = jnp.square(jnp.maximum(a, 0.0)).astype(BF16)
    o_ref[...] += jnp.dot(a, wdn_ref[...], preferred_element_type=F32)

    @pl.when(f == pl.num_programs(1) - 1)
    def _():
        o_ref[...] = _rms(o_ref[...], fg_ref[...])


def _mlp(h, g, wup, wdn, fg, tm, tf):
    m, d = h.shape
    ff = wup.shape[1]
    est = 2 * tm * d * 4 * 2 + tm * d * 2 + 2 * 2 * d * tf * 2 + tm * tf * 6 + tm * d * 4
    return pl.pallas_call(
        _mlp_kernel,
        grid=(m // tm, ff // tf),
        in_specs=[
            pl.BlockSpec((tm, d), lambda i, f: (i, 0)),
            pl.BlockSpec((1, d), lambda i, f: (0, 0)),
            pl.BlockSpec((d, tf), lambda i, f: (0, f)),
            pl.BlockSpec((tf, d), lambda i, f: (f, 0)),
            pl.BlockSpec((1, d), lambda i, f: (0, 0)),
        ],
        out_specs=pl.BlockSpec((tm, d), lambda i, f: (i, 0)),
        out_shape=jax.ShapeDtypeStruct((m, d), F32),
        scratch_shapes=[pltpu.VMEM((tm, d), BF16)],
        compiler_params=pltpu.CompilerParams(
            dimension_semantics=("arbitrary", "arbitrary"),
            vmem_limit_bytes=_vmem_limit(est + (8 << 20))),
        name="mlp",
    )(h, g, wup, wdn, fg)


def _rope_tables(pos, dim):
    inv = 1.0 / (ROPE_BASE ** (jnp.arange(0, dim, 2, dtype=F32) / dim))
    ang = pos[:, None] * inv[None, :]
    return jnp.cos(ang), jnp.sin(ang)


def kernel(x, meta_tokens, norm_mix_g, w_in, ret_out_g, q_norm_g, w_uq, kv_norm_g,
           w_uk, w_uv, mla_out_g, w_o, norm_mlp_g, w_up, w_down, final_norm_g):
    B, S, D = x.shape
    assert norm_mix_g.shape[0] == 1 and meta_tokens.shape == (N_META, D)
    M = B * S
    xm = x.reshape(M, D)

    w_in0 = w_in[0]
    w_in_aug = jnp.concatenate(
        [w_in0, w_in0[:, COL_KR:COL_KR + MLA_ROPE_DIM],
         jnp.zeros((D, PROJ_WIDTH - w_in0.shape[1] - MLA_ROPE_DIM), w_in0.dtype)], axis=1).astype(BF16)
    wq3 = w_uq[0].reshape(Q_LORA, MLA_HEADS, MLA_QK_DIM)
    wuq = jnp.concatenate(
        [wq3, jnp.zeros((Q_LORA, MLA_HEADS, HEAD_PAD - MLA_QK_DIM), wq3.dtype)],
        axis=2).reshape(Q_LORA, MLA_HEADS * HEAD_PAD).astype(BF16)
    wukt = w_uk[0].T.astype(BF16)
    sel = jnp.zeros((LANES, LANES), F32).at[:MLA_ROPE_DIM, :MLA_ROPE_DIM].set(
        jnp.eye(MLA_ROPE_DIM, dtype=F32)).astype(BF16)
    wuv = jnp.transpose(w_uv[0].reshape(KV_LORA, MLA_HEADS, MLA_V_DIM), (1, 0, 2)).astype(BF16)
    wo1 = w_o[0, :RET_WIDTH].astype(BF16)
    wo2 = w_o[0, RET_WIDTH:].astype(BF16)
    wup = w_up[0].astype(BF16)
    wdn = w_down[0].astype(BF16)

    pos = jnp.arange(S, dtype=F32) + float(N_META)
    posm = jnp.arange(N_META, dtype=F32)

    def ret_tables(p):
        c, s = _rope_tables(p, RET_HEAD_DIM)
        return jnp.concatenate([c, c], 1), jnp.concatenate([-s, s], 1)

    def mla_tables(p):
        c, s = _rope_tables(p, MLA_ROPE_DIM)
        return jnp.concatenate([c, c, c, c], 1), jnp.concatenate([-s, s, -s, s], 1)

    cos2, sin2 = ret_tables(pos)
    cosm, sinm = ret_tables(posm)
    pc, ps = mla_tables(pos)
    pcm, psm = mla_tables(posm)

    g_mix = norm_mix_g[0].reshape(1, D)
    proj = _inproj(xm, g_mix, w_in_aug, tm=1024, tn=1280)
    proj_meta = _inproj(meta_tokens, g_mix, w_in_aug, tm=N_META, tn=1280)
    projv = proj.reshape(B, S, PROJ_WIDTH)

    ret = _retention(projv, proj_meta, cos2, sin2, cosm, sinm,
                     ret_out_g[0].reshape(1, RET_WIDTH), T=256)
    mla = _mla(projv, proj_meta, pc, ps, pcm, psm,
               q_norm_g[0].reshape(1, Q_LORA), kv_norm_g[0].reshape(1, KV_LORA),
               wuq, wukt, sel, wuv, mla_out_g[0].reshape(1, MLA_WIDTH), TQ=256, G=8)

    h = _oproj(xm, ret.reshape(M, RET_WIDTH), mla.reshape(M, MLA_WIDTH), wo1, wo2, tm=512)
    out = _mlp(h, norm_mlp_g[0].reshape(1, D), wup, wdn, final_norm_g.reshape(1, D),
               tm=1024, tf=512)
    return out.reshape(B, S, D)
```

```python
import functools

import numpy as np
import jax
import jax.numpy as jnp
from jax import lax
from jax.experimental import pallas as pl
from jax.experimental.pallas import tpu as pltpu

F32 = jnp.float32
BF16 = jnp.bfloat16

CHUNK = 64
N_META = 16
RET_HEADS = 8
RET_HEAD_DIM = 128
MLA_HEADS = 8
MLA_V_DIM = 128
MLA_NOPE_DIM = 128
MLA_ROPE_DIM = 64
Q_LORA = 512
KV_LORA = 256
MLA_QK_DIM = MLA_NOPE_DIM + MLA_ROPE_DIM
ROPE_BASE = 10000.0
EPS = 1e-6
LOG2E = float(np.log2(np.e))

LANES = 128
RET_WIDTH = RET_HEADS * RET_HEAD_DIM
MLA_WIDTH = MLA_HEADS * MLA_V_DIM
COL_CQ = 4 * RET_WIDTH
COL_CKV = COL_CQ + Q_LORA
COL_KR = COL_CKV + KV_LORA
PROJ_WIDTH = 5120
KAUG = KV_LORA + LANES
HEAD_PAD = 256

VMEM_LIMIT_CAP = 60000 * 1024


def _vmem_limit(nbytes):
    return int(min(VMEM_LIMIT_CAP, nbytes))


def _rms(x, g):
    ms = jnp.mean(x * x, axis=-1, keepdims=True)
    return x * lax.rsqrt(ms + EPS) * g


def _rope128(x, cos2, sin2):
    return x * cos2 + pltpu.roll(x, 64, 1) * sin2


def _inproj_kernel(x_ref, g_ref, w_ref, cos_ref, sin_ref, o_ref, u_ref):
    j = pl.program_id(1)

    def rope_heads(acc, scale):
        cos2 = cos_ref[...]
        sin2 = sin_ref[...]
        return jnp.concatenate(
            [_rope128(acc[:, h * LANES:(h + 1) * LANES], cos2, sin2) * scale
             for h in range(RET_HEADS)], axis=1)

    @pl.when(j == 0)
    def _():
        u = _rms(x_ref[...], g_ref[...]).astype(BF16)
        u_ref[...] = u
        acc = jnp.dot(u, w_ref[...], preferred_element_type=F32)
        o_ref[...] = rope_heads(acc, RET_HEAD_DIM ** -0.5).astype(o_ref.dtype)

    @pl.when(j == 1)
    def _():
        acc = jnp.dot(u_ref[...], w_ref[...], preferred_element_type=F32)
        o_ref[...] = rope_heads(acc, 1.0).astype(o_ref.dtype)

    @pl.when(j == 3)
    def _():
        acc = jnp.dot(u_ref[...], w_ref[...], preferred_element_type=F32)
        o_ref[...] = (acc * (1.0 / (1.0 + jnp.exp(-acc)))).astype(o_ref.dtype)

    @pl.when((j == 2) | (j == 4))
    def _():
        o_ref[...] = jnp.dot(u_ref[...], w_ref[...],
                             preferred_element_type=F32).astype(o_ref.dtype)


def _inproj(x, g, w, cos2, sin2, tm):
    m, d = x.shape
    n = w.shape[1]
    tn = RET_WIDTH
    assert n == 5 * tn and cos2.shape[0] % tm == 0
    seq_tiles = cos2.shape[0] // tm
    est = (2 * tm * d * 4 + tm * d * 2 + 2 * d * tn * 2 + 2 * tm * tn * 2 + 2 * tm * tn * 4
           + tm * d * 4 + 4 * tm * LANES * 4)
    return pl.pallas_call(
        _inproj_kernel,
        grid=(m // tm, n // tn),
        in_specs=[
            pl.BlockSpec((tm, d), lambda i, j: (i, 0)),
            pl.BlockSpec((1, d), lambda i, j: (0, 0)),
            pl.BlockSpec((d, tn), lambda i, j: (0, j)),
            pl.BlockSpec((tm, LANES), lambda i, j: (i % seq_tiles, 0)),
            pl.BlockSpec((tm, LANES), lambda i, j: (i % seq_tiles, 0)),
        ],
        out_specs=pl.BlockSpec((tm, tn), lambda i, j: (i, j)),
        out_shape=jax.ShapeDtypeStruct((m, n), BF16),
        scratch_shapes=[pltpu.VMEM((tm, d), BF16)],
        compiler_params=pltpu.CompilerParams(
            dimension_semantics=("arbitrary", "arbitrary"),
            vmem_limit_bytes=_vmem_limit(est + (8 << 20))),
        name="inproj",
    )(x, g, w, cos2, sin2)


def _retention_kernel(q_ref, k_ref, v_ref, g_ref, km_ref, vm_ref, gout_ref,
                      o_ref,
                      state_ref, d_ref, wq_ref, wk_ref, *, T):
    b = pl.program_id(0)
    t = pl.program_id(1)
    H = RET_HEADS
    dk = RET_HEAD_DIM

    def log_gamma(h):
        e = jnp.full((1, 1), -5.0 - h, F32)
        return jnp.log(1.0 - jnp.exp2(e))

    @pl.when((b == 0) & (t == 0))
    def _():
        ii = lax.broadcasted_iota(jnp.int32, (T, T), 0)
        jj = lax.broadcasted_iota(jnp.int32, (T, T), 1)
        dist = jnp.abs(ii - jj).astype(F32)
        vis = (jj // CHUNK) <= (ii // CHUNK)
        row = lax.broadcasted_iota(jnp.int32, (T, LANES), 0).astype(F32)
        for h in range(H):
            lg = log_gamma(h)
            d_ref[h] = jnp.where(vis, jnp.exp(lg * dist), 0.0)
            wq_ref[h] = jnp.exp(lg * (row + 1.0))
            wk_ref[h] = jnp.exp(lg * (T - 1.0 - row))

    @pl.when(t == 0)
    def _():
        rowm = lax.broadcasted_iota(jnp.int32, (N_META, LANES), 0).astype(F32)
        for h in range(H):
            lg = log_gamma(h)
            km = km_ref[:, h * dk:(h + 1) * dk].astype(F32) * jnp.exp(lg * (N_META - 1.0 - rowm))
            vm = vm_ref[:, h * dk:(h + 1) * dk]
            state_ref[h] = lax.dot_general(
                km.astype(BF16), vm, (((0,), (0,)), ((), ())),
                preferred_element_type=F32)

    pre = []
    for h in range(H):
        cs = slice(h * dk, (h + 1) * dk)
        lg = log_gamma(h)
        q = q_ref[0, :, cs]
        k = k_ref[0, :, cs]
        v = v_ref[0, :, cs]
        s = lax.dot_general(q, k, (((1,), (1,)), ((), ())), preferred_element_type=F32)
        st = state_ref[h]
        inter = jnp.dot((q.astype(F32) * wq_ref[h]).astype(BF16), st.astype(BF16),
                        preferred_element_type=F32)
        kv = lax.dot_general((k.astype(F32) * wk_ref[h]).astype(BF16), v,
                             (((0,), (0,)), ((), ())), preferred_element_type=F32)
        state_ref[h] = st * jnp.exp(lg * float(T)) + kv
        pre.append((s, inter))
    for h in range(H):
        cs = slice(h * dk, (h + 1) * dk)
        s, inter = pre[h]
        sd = (s * d_ref[h]).astype(BF16)
        o = jnp.dot(sd, v_ref[0, :, cs], preferred_element_type=F32) + inter
        mu = jnp.mean(o, axis=-1, keepdims=True)
        oc = o - mu
        var = jnp.mean(oc * oc, axis=-1, keepdims=True)
        y = oc * lax.rsqrt(var + EPS) * gout_ref[:, cs]
        o_ref[0, :, cs] = (g_ref[0, :, cs].astype(F32) * y).astype(o_ref.dtype)


def _retention(projv, proj_meta, gout, T):
    B, S, _ = projv.shape
    W = RET_WIDTH
    est = (2 * 4 * T * W * 2 + 2 * T * W * 2
           + RET_HEADS * (T * T + 2 * T * LANES + 128 * 128) * 4 + 16 * T * T * 4)
    kern = functools.partial(_retention_kernel, T=T)
    return pl.pallas_call(
        kern,
        grid=(B, S // T),
        in_specs=[
            pl.BlockSpec((1, T, W), lambda b, t: (b, t, 0)),
            pl.BlockSpec((1, T, W), lambda b, t: (b, t, 1)),
            pl.BlockSpec((1, T, W), lambda b, t: (b, t, 2)),
            pl.BlockSpec((1, T, W), lambda b, t: (b, t, 3)),
            pl.BlockSpec((N_META, W), lambda b, t: (0, 1)),
            pl.BlockSpec((N_META, W), lambda b, t: (0, 2)),
            pl.BlockSpec((1, W), lambda b, t: (0, 0)),
        ],
        out_specs=pl.BlockSpec((1, T, W), lambda b, t: (b, t, 0)),
        out_shape=jax.ShapeDtypeStruct((B, S, W), BF16),
        scratch_shapes=[
            pltpu.VMEM((RET_HEADS, RET_HEAD_DIM, RET_HEAD_DIM), F32),
            pltpu.VMEM((RET_HEADS, T, T), F32),
            pltpu.VMEM((RET_HEADS, T, LANES), F32),
            pltpu.VMEM((RET_HEADS, T, LANES), F32),
        ],
        compiler_params=pltpu.CompilerParams(
            dimension_semantics=("arbitrary", "arbitrary"),
            vmem_limit_bytes=_vmem_limit(est + (8 << 20))),
        name="retention",
    )(projv, projv, projv, projv, proj_meta, proj_meta, gout)


def _swap32(x):
    lane = lax.broadcasted_iota(jnp.int32, x.shape, 1)
    return jnp.where((lane % 64) < 32, pltpu.roll(x, 96, 1), pltpu.roll(x, 32, 1))


def _rope64x2(x, pc, ps):
    return x * pc + _swap32(x) * ps


def _mla_kernel(cq_ref, kv_ref, kvm_ref, pc_ref, ps_ref, pcm_ref, psm_ref,
                qg_ref, kvg_ref, wuq_ref, wukt_ref, sel_ref, wuv_ref, og_ref,
                o_ref,
                kaug_ref, kt_ref, qh_ref, p_ref, m_ref, l_ref, a_ref, acc_ref,
                *, TQ, TK, G):
    i = pl.program_id(1)
    H = MLA_HEADS
    NU = kt_ref.shape[0]
    qscale = (MLA_QK_DIM ** -0.5) * LOG2E
    NEG = -1e30

    def prep_keys(kv, pc, ps):
        c = _rms(kv[:, :KV_LORA], kvg_ref[...])
        kr = _rope64x2(kv[:, KV_LORA:], pc, ps)
        return c.astype(BF16), kr.astype(BF16)

    @pl.when(i == 0)
    def _():
        kaug_ref[0:TK, :] = jnp.zeros((TK, KAUG), BF16)
        cm, krm = prep_keys(kvm_ref[...].astype(F32), pcm_ref[...], psm_ref[...])
        kaug_ref[0:N_META, 0:KV_LORA] = cm
        kaug_ref[0:N_META, KV_LORA:] = krm
        c, kr = prep_keys(kv_ref[0].astype(F32), pc_ref[...], ps_ref[...])
        kaug_ref[TK:, 0:KV_LORA] = c
        kaug_ref[TK:, KV_LORA:] = kr
        nt = (((1,), (1,)), ((), ()))
        for u in range(NU):
            ku = kaug_ref[u * TK:(u + 1) * TK, :]
            kn = lax.dot_general(wukt_ref[...], ku[:, 0:KV_LORA], nt,
                                 preferred_element_type=F32).astype(BF16)
            krt = lax.dot_general(sel_ref[...], ku[:, KV_LORA:], nt,
                                  preferred_element_type=F32).astype(BF16)
            for h in range(H):
                kt_ref[u, h * HEAD_PAD:h * HEAD_PAD + LANES, :] = kn[h * LANES:(h + 1) * LANES]
                kt_ref[u, h * HEAD_PAD + LANES:(h + 1) * HEAD_PAD, :] = krt

    q0 = pl.multiple_of(i * TQ, TQ)
    cq = _rms(cq_ref[0].astype(F32), qg_ref[...]).astype(BF16)
    q = jnp.dot(cq, wuq_ref[...], preferred_element_type=F32)
    pcq = pc_ref[pl.ds(q0, TQ), :]
    psq = ps_ref[pl.ds(q0, TQ), :]
    for h in range(H):
        c0 = h * HEAD_PAD
        qh_ref[h * TQ:(h + 1) * TQ, 0:LANES] = (q[:, c0:c0 + LANES] * qscale).astype(BF16)
        rp = _rope64x2(q[:, c0 + LANES:c0 + HEAD_PAD], pcq, psq) * qscale
        qh_ref[h * TQ:(h + 1) * TQ, LANES:] = rp.astype(BF16)

    def block(units, bias, first):
        W = sum(w for _, w in units)
        def key_rows(u, w):
            if isinstance(u, int):
                return kaug_ref[u * TK:u * TK + w, 0:KV_LORA]
            return kaug_ref[pl.ds(pl.multiple_of(u * TK, TK), w), 0:KV_LORA]

        vb = jnp.concatenate([key_rows(u, w) for u, w in units], axis=0)
        HG = H // G
        AHEAD = 2

        def scores(g):
            out = []
            for h in range(g * HG, (g + 1) * HG):
                qh = qh_ref[h * TQ:(h + 1) * TQ, :]
                out.append([jnp.dot(qh, kt_ref[u, h * HEAD_PAD:(h + 1) * HEAD_PAD, 0:w],
                                    preferred_element_type=F32) for u, w in units])
            return out

        s_groups = {g: scores(g) for g in range(min(AHEAD, G))}
        for g in range(G):
            gs = slice(g * HG * TQ, (g + 1) * HG * TQ)
            s_g = s_groups.pop(g)
            for hh in range(HG):
                h = g * HG + hh
                rs = slice(h * TQ, (h + 1) * TQ)
                cols = []
                for ui, s in enumerate(s_g[hh]):
                    if bias is not None and bias[ui] is not None:
                        s = s + bias[ui]
                    cols += [s[:, c * LANES:(c + 1) * LANES] for c in range(s.shape[1] // LANES)]
                m_cur = jnp.max(functools.reduce(jnp.maximum, cols), axis=-1, keepdims=True)
                if first:
                    m_new = jnp.broadcast_to(m_cur, (TQ, LANES))
                else:
                    m_prev = m_ref[rs]
                    m_new = jnp.maximum(m_prev, m_cur)
                    alpha = jnp.exp2(m_prev - m_new)
                    a_ref[rs] = alpha
                ps = [jnp.exp2(c - m_new) for c in cols]
                psum = functools.reduce(lambda x, y: x + y, ps)
                if first:
                    l_ref[rs] = psum
                else:
                    l_ref[rs] = alpha * l_ref[rs] + psum
                m_ref[rs] = m_new
                for c, pc_ in enumerate(ps):
                    p_ref[rs, c * LANES:(c + 1) * LANES] = pc_.astype(BF16)
            o = jnp.dot(p_ref[gs, 0:W], vb, preferred_element_type=F32)
            if first:
                acc_ref[gs, :] = o
            else:
                a = a_ref[gs, :]
                acc_ref[gs, 0:LANES] = a * acc_ref[gs, 0:LANES] + o[:, 0:LANES]
                acc_ref[gs, LANES:] = a * acc_ref[gs, LANES:] + o[:, LANES:]
            if g + AHEAD < G:
                s_groups[g + AHEAD] = scores(g + AHEAD)

    colh = lax.broadcasted_iota(jnp.int32, (TQ, TK), 1)
    bias_h = jnp.where(lax.broadcasted_iota(jnp.int32, (TQ, LANES), 1) < N_META, 0.0, NEG).astype(F32)
    rq = lax.broadcasted_iota(jnp.int32, (TQ, TK), 0) // CHUNK
    bias_d = jnp.where(colh // CHUNK <= rq, 0.0, NEG).astype(F32)

    block([(0, LANES), (i + 1, TK)], [bias_h, bias_d], True)

    def body(t, carry):
        block([(2 * t + 1, TK), (2 * t + 2, TK)], None, False)
        return carry

    lax.fori_loop(0, i // 2, body, 0)

    @pl.when(i % 2 == 1)
    def _():
        block([(i, TK)], None, False)

    outs = []
    for h in range(H):
        rs = slice(h * TQ, (h + 1) * TQ)
        inv_l = 1.0 / jnp.sum(l_ref[rs], axis=-1, keepdims=True)
        o = (acc_ref[rs] * inv_l).astype(BF16)
        outs.append(jnp.dot(o, wuv_ref[h], preferred_element_type=F32))
    mla = jnp.concatenate(outs, axis=-1)
    o_ref[0] = _rms(mla, og_ref[...]).astype(o_ref.dtype)


def _mla(projv, proj_meta, pc, ps, pcm, psm, qg, kvg, wuq, wukt, sel, wuv, og, TQ, G):
    B, S, _ = projv.shape
    TK = TQ
    NU = 1 + S // TK
    R = MLA_HEADS * TQ
    nq = wuq.shape[1]
    HP = MLA_HEADS * HEAD_PAD
    est = (2 * TQ * Q_LORA * 2 + 2 * S * KAUG * 2 + 4 * S * LANES * 4
           + 2 * (Q_LORA * nq + 2 * MLA_HEADS * 128 * KV_LORA + LANES * LANES) * 2 + 2 * TQ * MLA_WIDTH * 2
           + NU * TK * KAUG * 2 + NU * HP * TK * 2 + R * HEAD_PAD * 2 + R * 2 * TK * 2
           + 3 * R * LANES * 4 + R * KV_LORA * 4
           + 3 * R * TK * 4 + S * KAUG * 4 * 2)
    kern = functools.partial(_mla_kernel, TQ=TQ, TK=TK, G=G)
    const2 = lambda b, i: (0, 0)
    const3 = lambda b, i: (0, 0, 0)
    return pl.pallas_call(
        kern,
        grid=(B, S // TQ),
        in_specs=[
            pl.BlockSpec((1, TQ, Q_LORA), lambda b, i: (b, i, COL_CQ // Q_LORA)),
            pl.BlockSpec((1, S, KAUG), lambda b, i: (b, 0, COL_CKV // KAUG)),
            pl.BlockSpec((N_META, KAUG), lambda b, i: (0, COL_CKV // KAUG)),
            pl.BlockSpec((S, LANES), const2),
            pl.BlockSpec((S, LANES), const2),
            pl.BlockSpec((N_META, LANES), const2),
            pl.BlockSpec((N_META, LANES), const2),
            pl.BlockSpec((1, Q_LORA), const2),
            pl.BlockSpec((1, KV_LORA), const2),
            pl.BlockSpec((Q_LORA, nq), const2),
            pl.BlockSpec((MLA_HEADS * MLA_NOPE_DIM, KV_LORA), const2),
            pl.BlockSpec((LANES, LANES), const2),
            pl.BlockSpec((MLA_HEADS, KV_LORA, MLA_V_DIM), const3),
            pl.BlockSpec((1, MLA_WIDTH), const2),
        ],
        out_specs=pl.BlockSpec((1, TQ, MLA_WIDTH), lambda b, i: (b, i, 0)),
        out_shape=jax.ShapeDtypeStruct((B, S, MLA_WIDTH), BF16),
        scratch_shapes=[
            pltpu.VMEM((NU * TK, KAUG), BF16),
            pltpu.VMEM((NU, HP, TK), BF16),
            pltpu.VMEM((R, HEAD_PAD), BF16),
            pltpu.VMEM((R, 2 * TK), BF16),
            pltpu.VMEM((R, LANES), F32),
            pltpu.VMEM((R, LANES), F32),
            pltpu.VMEM((R, LANES), F32),
            pltpu.VMEM((R, KV_LORA), F32),
        ],
        compiler_params=pltpu.CompilerParams(
            dimension_semantics=("arbitrary", "arbitrary"),
            vmem_limit_bytes=_vmem_limit(est + (8 << 20))),
        name="mla",
    )(projv, projv, proj_meta, pc, ps, pcm, psm, qg, kvg, wuq, wukt, sel, wuv, og)


def _oproj_kernel(x_ref, a1_ref, a2_ref, w1_ref, w2_ref, o_ref):
    acc = jnp.dot(a1_ref[...], w1_ref[...], preferred_element_type=F32)
    acc = acc + jnp.dot(a2_ref[...], w2_ref[...], preferred_element_type=F32)
    o_ref[...] = x_ref[...] + acc


def _oproj(x, a1, a2, w1, w2, tm):
    m, d = x.shape
    k = a1.shape[1]
    est = 2 * tm * d * 4 * 2 + 2 * 2 * tm * k * 2 + 2 * 2 * k * d * 2 + tm * d * 4
    return pl.pallas_call(
        _oproj_kernel,
        grid=(m // tm,),
        in_specs=[
            pl.BlockSpec((tm, d), lambda i: (i, 0)),
            pl.BlockSpec((tm, k), lambda i: (i, 0)),
            pl.BlockSpec((tm, k), lambda i: (i, 0)),
            pl.BlockSpec((k, d), lambda i: (0, 0)),
            pl.BlockSpec((k, d), lambda i: (0, 0)),
        ],
        out_specs=pl.BlockSpec((tm, d), lambda i: (i, 0)),
        out_shape=jax.ShapeDtypeStruct((m, d), F32),
        compiler_params=pltpu.CompilerParams(
            dimension_semantics=("arbitrary",),
            vmem_limit_bytes=_vmem_limit(est + (8 << 20))),
        name="oproj",
    )(x, a1, a2, w1, w2)


def _mlp_kernel(h_ref, g_ref, wup_ref, wdn_ref, fg_ref, o_ref, m_ref):
    f = pl.program_id(1)
    last = pl.num_programs(1) - 1

    def ffn(m):
        a = jnp.dot(m, wup_ref[...], preferred_element_type=F32)
        a = jnp.square(jnp.maximum(a, 0.0)).astype(BF16)
        return jnp.dot(a, wdn_ref[...], preferred_element_type=F32)

    @pl.when(f == 0)
    def _():
        m = _rms(h_ref[...], g_ref[...]).astype(BF16)
        m_ref[...] = m
        o_ref[...] = ffn(m)

    @pl.when((f > 0) & (f < last))
    def _():
        o_ref[...] += ffn(m_ref[...])

    @pl.when(f == last)
    def _():
        y = o_ref[...] + ffn(m_ref[...]) + h_ref[...]
        o_ref[...] = _rms(y, fg_ref[...])


def _mlp(h, g, wup, wdn, fg, tm, tf):
    m, d = h.shape
    ff = wup.shape[1]
    est = 2 * tm * d * 4 * 2 + tm * d * 2 + 2 * 2 * d * tf * 2 + tm * tf * 6 + tm * d * 4
    return pl.pallas_call(
        _mlp_kernel,
        grid=(m // tm, ff // tf),
        in_specs=[
            pl.BlockSpec((tm, d), lambda i, f: (i, 0)),
            pl.BlockSpec((1, d), lambda i, f: (0, 0)),
            pl.BlockSpec((d, tf), lambda i, f: (0, f)),
            pl.BlockSpec((tf, d), lambda i, f: (f, 0)),
            pl.BlockSpec((1, d), lambda i, f: (0, 0)),
        ],
        out_specs=pl.BlockSpec((tm, d), lambda i, f: (i, 0)),
        out_shape=jax.ShapeDtypeStruct((m, d), F32),
        scratch_shapes=[pltpu.VMEM((tm, d), BF16)],
        compiler_params=pltpu.CompilerParams(
            dimension_semantics=("arbitrary", "arbitrary"),
            vmem_limit_bytes=_vmem_limit(est + (8 << 20))),
        name="mlp",
    )(h, g, wup, wdn, fg)


def _rope_tables(pos, dim):
    inv = 1.0 / (ROPE_BASE ** (jnp.arange(0, dim, 2, dtype=F32) / dim))
    ang = pos[:, None] * inv[None, :]
    return jnp.cos(ang), jnp.sin(ang)


def kernel(x, meta_tokens, norm_mix_g, w_in, ret_out_g, q_norm_g, w_uq, kv_norm_g,
           w_uk, w_uv, mla_out_g, w_o, norm_mlp_g, w_up, w_down, final_norm_g):
    B, S, D = x.shape
    assert norm_mix_g.shape[0] == 1 and meta_tokens.shape == (N_META, D)
    M = B * S
    xm = x.reshape(M, D)

    w_in0 = w_in[0]
    w_in_aug = jnp.concatenate(
        [w_in0, w_in0[:, COL_KR:COL_KR + MLA_ROPE_DIM],
         jnp.zeros((D, PROJ_WIDTH - w_in0.shape[1] - MLA_ROPE_DIM), w_in0.dtype)], axis=1).astype(BF16)
    wq3 = w_uq[0].reshape(Q_LORA, MLA_HEADS, MLA_QK_DIM)
    wuq = jnp.concatenate(
        [wq3, jnp.zeros((Q_LORA, MLA_HEADS, HEAD_PAD - MLA_QK_DIM), wq3.dtype)],
        axis=2).reshape(Q_LORA, MLA_HEADS * HEAD_PAD).astype(BF16)
    wukt = w_uk[0].T.astype(BF16)
    sel = jnp.zeros((LANES, LANES), F32).at[:MLA_ROPE_DIM, :MLA_ROPE_DIM].set(
        jnp.eye(MLA_ROPE_DIM, dtype=F32)).astype(BF16)
    wuv = jnp.transpose(w_uv[0].reshape(KV_LORA, MLA_HEADS, MLA_V_DIM), (1, 0, 2)).astype(BF16)
    wo1 = w_o[0, :RET_WIDTH].astype(BF16)
    wo2 = w_o[0, RET_WIDTH:].astype(BF16)
    wup = w_up[0].astype(BF16)
    wdn = w_down[0].astype(BF16)

    pos = jnp.arange(S, dtype=F32) + float(N_META)
    posm = jnp.arange(N_META, dtype=F32)

    def ret_tables(p):
        c, s = _rope_tables(p, RET_HEAD_DIM)
        return jnp.concatenate([c, c], 1), jnp.concatenate([-s, s], 1)

    def mla_tables(p):
        c, s = _rope_tables(p, MLA_ROPE_DIM)
        return jnp.concatenate([c, c, c, c], 1), jnp.concatenate([-s, s, -s, s], 1)

    cos2, sin2 = ret_tables(pos)
    cosm, sinm = ret_tables(posm)
    pc, ps = mla_tables(pos)
    pcm, psm = mla_tables(posm)

    g_mix = norm_mix_g[0].reshape(1, D)
    proj = _inproj(xm, g_mix, w_in_aug, cos2, sin2, tm=1024)
    proj_meta = _inproj(meta_tokens, g_mix, w_in_aug, cosm, sinm, tm=N_META)
    projv = proj.reshape(B, S, PROJ_WIDTH)

    ret = _retention(projv, proj_meta, ret_out_g[0].reshape(1, RET_WIDTH), T=256)
    mla = _mla(projv, proj_meta, pc, ps, pcm, psm,
               q_norm_g[0].reshape(1, Q_LORA), kv_norm_g[0].reshape(1, KV_LORA),
               wuq, wukt, sel, wuv, mla_out_g[0].reshape(1, MLA_WIDTH), TQ=256, G=8)

    h = _oproj(xm, ret.reshape(M, RET_WIDTH), mla.reshape(M, MLA_WIDTH), wo1, wo2, tm=512)
    out = _mlp(h, norm_mlp_g[0].reshape(1, D), wup, wdn, final_norm_g.reshape(1, D),
               tm=1024, tf=512)
    return out.reshape(B, S, D)
```

```python
import functools

import numpy as np
import jax
import jax.numpy as jnp
from jax import lax
from jax.experimental import pallas as pl
from jax.experimental.pallas import tpu as pltpu

F32 = jnp.float32
BF16 = jnp.bfloat16

CHUNK = 64
N_META = 16
RET_HEADS = 8
RET_HEAD_DIM = 128
MLA_HEADS = 8
MLA_V_DIM = 128
MLA_NOPE_DIM = 128
MLA_ROPE_DIM = 64
Q_LORA = 512
KV_LORA = 256
MLA_QK_DIM = MLA_NOPE_DIM + MLA_ROPE_DIM
ROPE_BASE = 10000.0
EPS = 1e-6
LOG2E = float(np.log2(np.e))

LANES = 128
RET_WIDTH = RET_HEADS * RET_HEAD_DIM
MLA_WIDTH = MLA_HEADS * MLA_V_DIM
COL_CQ = 4 * RET_WIDTH
COL_CKV = COL_CQ + Q_LORA
COL_KR = COL_CKV + KV_LORA
PROJ_WIDTH = 5120
KAUG = KV_LORA + LANES
HEAD_PAD = 256

VMEM_LIMIT_CAP = 60000 * 1024


def _vmem_limit(nbytes):
    return int(min(VMEM_LIMIT_CAP, nbytes))


def _rms(x, g):
    ms = jnp.mean(x * x, axis=-1, keepdims=True)
    return x * lax.rsqrt(ms + EPS) * g


def _rope128(x, cos2, sin2):
    return x * cos2 + pltpu.roll(x, 64, 1) * sin2


def _staggered_parts(n_tiles, nparts):
    def make(q):
        def index_map(i, s):
            nxt = jnp.minimum(i + (s > q).astype(jnp.int32), n_tiles - 1)
            return (nparts * nxt + q, 0)
        return index_map
    return [make(q) for q in range(nparts)]


def _inproj_kernel(*refs, nparts, nchunks):
    x_refs = refs[:nparts]
    g_ref, w_ref, cos_ref, sin_ref, o_ref, u_ref = refs[nparts:]
    j = pl.program_id(1)
    tm = u_ref.shape[0]
    tp = tm // nparts
    rc = tm // nchunks

    def rope_heads(scale):
        def epilogue(acc, rows):
            cos2 = cos_ref[rows, :]
            sin2 = sin_ref[rows, :]
            return jnp.concatenate(
                [_rope128(acc[:, h * LANES:(h + 1) * LANES], cos2, sin2) * scale
                 for h in range(RET_HEADS)], axis=1)
        return epilogue

    def project(epilogue):
        for c in range(nchunks):
            rows = slice(c * rc, (c + 1) * rc)
            acc = jnp.dot(u_ref[rows, :], w_ref[...], preferred_element_type=F32)
            o_ref[rows, :] = epilogue(acc, rows).astype(o_ref.dtype)

    @pl.when(j == 0)
    def _():
        for q in range(nparts):
            u_ref[q * tp:(q + 1) * tp, :] = _rms(x_refs[q][...], g_ref[...]).astype(BF16)
        project(rope_heads(RET_HEAD_DIM ** -0.5))

    @pl.when(j == 1)
    def _():
        project(rope_heads(1.0))

    @pl.when(j == 3)
    def _():
        project(lambda acc, rows: acc * (1.0 / (1.0 + jnp.exp(-acc))))

    @pl.when((j == 2) | (j == 4))
    def _():
        o_ref[...] = jnp.dot(u_ref[...], w_ref[...],
                             preferred_element_type=F32).astype(o_ref.dtype)


def _inproj(x, g, w, cos2, sin2, tm, nparts, nchunks):
    m, d = x.shape
    n = w.shape[1]
    tn = RET_WIDTH
    assert n == 5 * tn and cos2.shape[0] % tm == 0 and nparts < n // tn
    seq_tiles = cos2.shape[0] // tm
    tp = tm // nparts
    est = (2 * tm * d * 4 + tm * d * 2 + 2 * d * tn * 2 + 2 * tm * tn * 2 + 2 * tm * tn * 4
           + tm * d * 4 + 4 * tm * LANES * 4)
    kern = functools.partial(_inproj_kernel, nparts=nparts, nchunks=nchunks)
    return pl.pallas_call(
        kern,
        grid=(m // tm, n // tn),
        in_specs=[pl.BlockSpec((tp, d), im) for im in _staggered_parts(m // tm, nparts)] + [
            pl.BlockSpec((1, d), lambda i, j: (0, 0)),
            pl.BlockSpec((d, tn), lambda i, j: (0, j)),
            pl.BlockSpec((tm, LANES), lambda i, j: (i % seq_tiles, 0)),
            pl.BlockSpec((tm, LANES), lambda i, j: (i % seq_tiles, 0)),
        ],
        out_specs=pl.BlockSpec((tm, tn), lambda i, j: (i, j)),
        out_shape=jax.ShapeDtypeStruct((m, n), BF16),
        scratch_shapes=[pltpu.VMEM((tm, d), BF16)],
        compiler_params=pltpu.CompilerParams(
            dimension_semantics=("arbitrary", "arbitrary"),
            vmem_limit_bytes=_vmem_limit(est + (8 << 20))),
        name="inproj",
    )(*([x] * nparts), g, w, cos2, sin2)


def _retention_kernel(q_ref, k_ref, v_ref, g_ref, km_ref, vm_ref, gout_ref,
                      o_ref,
                      state_ref, d_ref, wq_ref, wk_ref, *, T):
    b = pl.program_id(0)
    t = pl.program_id(1)
    H = RET_HEADS
    dk = RET_HEAD_DIM

    def log_gamma(h):
        e = jnp.full((1, 1), -5.0 - h, F32)
        return jnp.log(1.0 - jnp.exp2(e))

    @pl.when((b == 0) & (t == 0))
    def _():
        ii = lax.broadcasted_iota(jnp.int32, (T, T), 0)
        jj = lax.broadcasted_iota(jnp.int32, (T, T), 1)
        dist = jnp.abs(ii - jj).astype(F32)
        vis = (jj // CHUNK) <= (ii // CHUNK)
        row = lax.broadcasted_iota(jnp.int32, (T, LANES), 0).astype(F32)
        for h in range(H):
            lg = log_gamma(h)
            d_ref[h] = jnp.where(vis, jnp.exp(lg * dist), 0.0)
            wq_ref[h] = jnp.exp(lg * (row + 1.0))
            wk_ref[h] = jnp.exp(lg * (T - 1.0 - row))

    @pl.when(t == 0)
    def _():
        rowm = lax.broadcasted_iota(jnp.int32, (N_META, LANES), 0).astype(F32)
        for h in range(H):
            lg = log_gamma(h)
            km = km_ref[:, h * dk:(h + 1) * dk].astype(F32) * jnp.exp(lg * (N_META - 1.0 - rowm))
            vm = vm_ref[:, h * dk:(h + 1) * dk]
            state_ref[h] = lax.dot_general(
                km.astype(BF16), vm, (((0,), (0,)), ((), ())),
                preferred_element_type=F32)

    pre = []
    for h in range(H):
        cs = slice(h * dk, (h + 1) * dk)
        lg = log_gamma(h)
        q = q_ref[0, :, cs]
        k = k_ref[0, :, cs]
        v = v_ref[0, :, cs]
        s = lax.dot_general(q, k, (((1,), (1,)), ((), ())), preferred_element_type=F32)
        st = state_ref[h]
        inter = jnp.dot((q.astype(F32) * wq_ref[h]).astype(BF16), st.astype(BF16),
                        preferred_element_type=F32)
        kv = lax.dot_general((k.astype(F32) * wk_ref[h]).astype(BF16), v,
                             (((0,), (0,)), ((), ())), preferred_element_type=F32)
        state_ref[h] = st * jnp.exp(lg * float(T)) + kv
        pre.append((s, inter))
    for h in range(H):
        cs = slice(h * dk, (h + 1) * dk)
        s, inter = pre[h]
        sd = (s * d_ref[h]).astype(BF16)
        o = jnp.dot(sd, v_ref[0, :, cs], preferred_element_type=F32) + inter
        mu = jnp.mean(o, axis=-1, keepdims=True)
        oc = o - mu
        var = jnp.mean(oc * oc, axis=-1, keepdims=True)
        y = oc * lax.rsqrt(var + EPS) * gout_ref[:, cs]
        o_ref[0, :, cs] = (g_ref[0, :, cs].astype(F32) * y).astype(o_ref.dtype)


def _retention(projv, proj_meta, gout, T):
    B, S, _ = projv.shape
    W = RET_WIDTH
    est = (2 * 4 * T * W * 2 + 2 * T * W * 2
           + RET_HEADS * (T * T + 2 * T * LANES + 128 * 128) * 4 + 16 * T * T * 4)
    kern = functools.partial(_retention_kernel, T=T)
    return pl.pallas_call(
        kern,
        grid=(B, S // T),
        in_specs=[
            pl.BlockSpec((1, T, W), lambda b, t: (b, t, 0)),
            pl.BlockSpec((1, T, W), lambda b, t: (b, t, 1)),
            pl.BlockSpec((1, T, W), lambda b, t: (b, t, 2)),
            pl.BlockSpec((1, T, W), lambda b, t: (b, t, 3)),
            pl.BlockSpec((N_META, W), lambda b, t: (0, 1)),
            pl.BlockSpec((N_META, W), lambda b, t: (0, 2)),
            pl.BlockSpec((1, W), lambda b, t: (0, 0)),
        ],
        out_specs=pl.BlockSpec((1, T, W), lambda b, t: (b, t, 0)),
        out_shape=jax.ShapeDtypeStruct((B, S, W), BF16),
        scratch_shapes=[
            pltpu.VMEM((RET_HEADS, RET_HEAD_DIM, RET_HEAD_DIM), F32),
            pltpu.VMEM((RET_HEADS, T, T), F32),
            pltpu.VMEM((RET_HEADS, T, LANES), F32),
            pltpu.VMEM((RET_HEADS, T, LANES), F32),
        ],
        compiler_params=pltpu.CompilerParams(
            dimension_semantics=("arbitrary", "arbitrary"),
            vmem_limit_bytes=_vmem_limit(est + (8 << 20))),
        name="retention",
    )(projv, projv, projv, projv, proj_meta, proj_meta, gout)


def _swap32(x):
    lane = lax.broadcasted_iota(jnp.int32, x.shape, 1)
    return jnp.where((lane % 64) < 32, pltpu.roll(x, 96, 1), pltpu.roll(x, 32, 1))


def _rope64x2(x, pc, ps):
    return x * pc + _swap32(x) * ps


def _mla_kernel(cq_ref, kv_ref, kvm_ref, pc_ref, ps_ref, pcm_ref, psm_ref,
                qg_ref, kvg_ref, wuq_ref, wukt_ref, sel_ref, wuv_ref, og_ref,
                o_ref,
                kaug_ref, kt_ref, qh_ref, p_ref, m_ref, l_ref, a_ref, acc_ref,
                *, TQ, TK, G):
    i = pl.program_id(1)
    H = MLA_HEADS
    NU = kt_ref.shape[0]
    qscale = (MLA_QK_DIM ** -0.5) * LOG2E
    NEG = -1e30

    def prep_keys(kv, pc, ps):
        c = _rms(kv[:, :KV_LORA], kvg_ref[...])
        kr = _rope64x2(kv[:, KV_LORA:], pc, ps)
        return c.astype(BF16), kr.astype(BF16)

    @pl.when(i == 0)
    def _():
        kaug_ref[0:TK, :] = jnp.zeros((TK, KAUG), BF16)
        cm, krm = prep_keys(kvm_ref[...].astype(F32), pcm_ref[...], psm_ref[...])
        kaug_ref[0:N_META, 0:KV_LORA] = cm
        kaug_ref[0:N_META, KV_LORA:] = krm
        c, kr = prep_keys(kv_ref[0].astype(F32), pc_ref[...], ps_ref[...])
        kaug_ref[TK:, 0:KV_LORA] = c
        kaug_ref[TK:, KV_LORA:] = kr
        nt = (((1,), (1,)), ((), ()))
        for u in range(NU):
            ku = kaug_ref[u * TK:(u + 1) * TK, :]
            kn = lax.dot_general(wukt_ref[...], ku[:, 0:KV_LORA], nt,
                                 preferred_element_type=F32).astype(BF16)
            krt = lax.dot_general(sel_ref[...], ku[:, KV_LORA:], nt,
                                  preferred_element_type=F32).astype(BF16)
            for h in range(H):
                kt_ref[u, h * HEAD_PAD:h * HEAD_PAD + LANES, :] = kn[h * LANES:(h + 1) * LANES]
                kt_ref[u, h * HEAD_PAD + LANES:(h + 1) * HEAD_PAD, :] = krt

    q0 = pl.multiple_of(i * TQ, TQ)
    cq = _rms(cq_ref[0].astype(F32), qg_ref[...]).astype(BF16)
    q = jnp.dot(cq, wuq_ref[...], preferred_element_type=F32)
    pcq = pc_ref[pl.ds(q0, TQ), :]
    psq = ps_ref[pl.ds(q0, TQ), :]
    for h in range(H):
        c0 = h * HEAD_PAD
        qh_ref[h * TQ:(h + 1) * TQ, 0:LANES] = (q[:, c0:c0 + LANES] * qscale).astype(BF16)
        rp = _rope64x2(q[:, c0 + LANES:c0 + HEAD_PAD], pcq, psq) * qscale
        qh_ref[h * TQ:(h + 1) * TQ, LANES:] = rp.astype(BF16)

    def block(units, bias, first):
        W = sum(w for _, w in units)
        def key_rows(u, w):
            if isinstance(u, int):
                return kaug_ref[u * TK:u * TK + w, 0:KV_LORA]
            return kaug_ref[pl.ds(pl.multiple_of(u * TK, TK), w), 0:KV_LORA]

        vb = jnp.concatenate([key_rows(u, w) for u, w in units], axis=0)
        HG = H // G
        AHEAD = 2

        def scores(g):
            out = []
            for h in range(g * HG, (g + 1) * HG):
                qh = qh_ref[h * TQ:(h + 1) * TQ, :]
                out.append([jnp.dot(qh, kt_ref[u, h * HEAD_PAD:(h + 1) * HEAD_PAD, 0:w],
                                    preferred_element_type=F32) for u, w in units])
            return out

        s_groups = {g: scores(g) for g in range(min(AHEAD, G))}
        for g in range(G):
            gs = slice(g * HG * TQ, (g + 1) * HG * TQ)
            s_g = s_groups.pop(g)
            for hh in range(HG):
                h = g * HG + hh
                rs = slice(h * TQ, (h + 1) * TQ)
                cols = []
                for ui, s in enumerate(s_g[hh]):
                    if bias is not None and bias[ui] is not None:
                        s = s + bias[ui]
                    cols += [s[:, c * LANES:(c + 1) * LANES] for c in range(s.shape[1] // LANES)]
                m_cur = jnp.max(functools.reduce(jnp.maximum, cols), axis=-1, keepdims=True)
                if first:
                    m_new = jnp.broadcast_to(m_cur, (TQ, LANES))
                else:
                    m_prev = m_ref[rs]
                    m_new = jnp.maximum(m_prev, m_cur)
                    alpha = jnp.exp2(m_prev - m_new)
                    a_ref[rs] = alpha
                ps = [jnp.exp2(c - m_new) for c in cols]
                psum = functools.reduce(lambda x, y: x + y, ps)
                if first:
                    l_ref[rs] = psum
                else:
                    l_ref[rs] = alpha * l_ref[rs] + psum
                m_ref[rs] = m_new
                for c, pc_ in enumerate(ps):
                    p_ref[rs, c * LANES:(c + 1) * LANES] = pc_.astype(BF16)
            o = jnp.dot(p_ref[gs, 0:W], vb, preferred_element_type=F32)
            if first:
                acc_ref[gs, :] = o
            else:
                a = a_ref[gs, :]
                acc_ref[gs, 0:LANES] = a * acc_ref[gs, 0:LANES] + o[:, 0:LANES]
                acc_ref[gs, LANES:] = a * acc_ref[gs, LANES:] + o[:, LANES:]
            if g + AHEAD < G:
                s_groups[g + AHEAD] = scores(g + AHEAD)

    colh = lax.broadcasted_iota(jnp.int32, (TQ, TK), 1)
    bias_h = jnp.where(lax.broadcasted_iota(jnp.int32, (TQ, LANES), 1) < N_META, 0.0, NEG).astype(F32)
    rq = lax.broadcasted_iota(jnp.int32, (TQ, TK), 0) // CHUNK
    bias_d = jnp.where(colh // CHUNK <= rq, 0.0, NEG).astype(F32)

    block([(0, LANES), (i + 1, TK)], [bias_h, bias_d], True)

    def body(t, carry):
        block([(2 * t + 1, TK), (2 * t + 2, TK)], None, False)
        return carry

    lax.fori_loop(0, i // 2, body, 0)

    @pl.when(i % 2 == 1)
    def _():
        block([(i, TK)], None, False)

    outs = []
    for h in range(H):
        rs = slice(h * TQ, (h + 1) * TQ)
        inv_l = 1.0 / jnp.sum(l_ref[rs], axis=-1, keepdims=True)
        o = (acc_ref[rs] * inv_l).astype(BF16)
        outs.append(jnp.dot(o, wuv_ref[h], preferred_element_type=F32))
    mla = jnp.concatenate(outs, axis=-1)
    o_ref[0] = _rms(mla, og_ref[...]).astype(o_ref.dtype)


def _mla(projv, proj_meta, pc, ps, pcm, psm, qg, kvg, wuq, wukt, sel, wuv, og, TQ, G):
    B, S, _ = projv.shape
    TK = TQ
    NU = 1 + S // TK
    R = MLA_HEADS * TQ
    nq = wuq.shape[1]
    HP = MLA_HEADS * HEAD_PAD
    est = (2 * TQ * Q_LORA * 2 + 2 * S * KAUG * 2 + 4 * S * LANES * 4
           + 2 * (Q_LORA * nq + 2 * MLA_HEADS * 128 * KV_LORA + LANES * LANES) * 2 + 2 * TQ * MLA_WIDTH * 2
           + NU * TK * KAUG * 2 + NU * HP * TK * 2 + R * HEAD_PAD * 2 + R * 2 * TK * 2
           + 3 * R * LANES * 4 + R * KV_LORA * 4
           + 3 * R * TK * 4 + S * KAUG * 4 * 2)
    kern = functools.partial(_mla_kernel, TQ=TQ, TK=TK, G=G)
    const2 = lambda b, i: (0, 0)
    const3 = lambda b, i: (0, 0, 0)
    return pl.pallas_call(
        kern,
        grid=(B, S // TQ),
        in_specs=[
            pl.BlockSpec((1, TQ, Q_LORA), lambda b, i: (b, i, COL_CQ // Q_LORA)),
            pl.BlockSpec((1, S, KAUG), lambda b, i: (b, 0, COL_CKV // KAUG)),
            pl.BlockSpec((N_META, KAUG), lambda b, i: (0, COL_CKV // KAUG)),
            pl.BlockSpec((S, LANES), const2),
            pl.BlockSpec((S, LANES), const2),
            pl.BlockSpec((N_META, LANES), const2),
            pl.BlockSpec((N_META, LANES), const2),
            pl.BlockSpec((1, Q_LORA), const2),
            pl.BlockSpec((1, KV_LORA), const2),
            pl.BlockSpec((Q_LORA, nq), const2),
            pl.BlockSpec((MLA_HEADS * MLA_NOPE_DIM, KV_LORA), const2),
            pl.BlockSpec((LANES, LANES), const2),
            pl.BlockSpec((MLA_HEADS, KV_LORA, MLA_V_DIM), const3),
            pl.BlockSpec((1, MLA_WIDTH), const2),
        ],
        out_specs=pl.BlockSpec((1, TQ, MLA_WIDTH), lambda b, i: (b, i, 0)),
        out_shape=jax.ShapeDtypeStruct((B, S, MLA_WIDTH), BF16),
        scratch_shapes=[
            pltpu.VMEM((NU * TK, KAUG), BF16),
            pltpu.VMEM((NU, HP, TK), BF16),
            pltpu.VMEM((R, HEAD_PAD), BF16),
            pltpu.VMEM((R, 2 * TK), BF16),
            pltpu.VMEM((R, LANES), F32),
            pltpu.VMEM((R, LANES), F32),
            pltpu.VMEM((R, LANES), F32),
            pltpu.VMEM((R, KV_LORA), F32),
        ],
        compiler_params=pltpu.CompilerParams(
            dimension_semantics=("arbitrary", "arbitrary"),
            vmem_limit_bytes=_vmem_limit(est + (8 << 20))),
        name="mla",
    )(projv, projv, proj_meta, pc, ps, pcm, psm, qg, kvg, wuq, wukt, sel, wuv, og)


def _oproj_kernel(x_ref, a1_ref, a2_ref, w1_ref, w2_ref, o_ref):
    acc = jnp.dot(a1_ref[...], w1_ref[...], preferred_element_type=F32)
    acc = acc + jnp.dot(a2_ref[...], w2_ref[...], preferred_element_type=F32)
    o_ref[...] = x_ref[...] + acc


def _oproj(x, a1, a2, w1, w2, tm):
    m, d = x.shape
    k = a1.shape[1]
    est = 2 * tm * d * 4 * 2 + 2 * 2 * tm * k * 2 + 2 * 2 * k * d * 2 + tm * d * 4
    return pl.pallas_call(
        _oproj_kernel,
        grid=(m // tm,),
        in_specs=[
            pl.BlockSpec((tm, d), lambda i: (i, 0)),
            pl.BlockSpec((tm, k), lambda i: (i, 0)),
            pl.BlockSpec((tm, k), lambda i: (i, 0)),
            pl.BlockSpec((k, d), lambda i: (0, 0)),
            pl.BlockSpec((k, d), lambda i: (0, 0)),
        ],
        out_specs=pl.BlockSpec((tm, d), lambda i: (i, 0)),
        out_shape=jax.ShapeDtypeStruct((m, d), F32),
        compiler_params=pltpu.CompilerParams(
            dimension_semantics=("arbitrary",),
            vmem_limit_bytes=_vmem_limit(est + (8 << 20))),
        name="oproj",
    )(x, a1, a2, w1, w2)


def _mlp_kernel(*refs, nparts, nchunks):
    h_refs = refs[:nparts]
    g_ref, wup_ref, wdn_ref, fg_ref, o_ref, m_ref = refs[nparts:]
    f = pl.program_id(1)
    last = pl.num_programs(1) - 1
    tm = m_ref.shape[0]
    tp = tm // nparts
    rc = tm // nchunks

    def ffn(m):
        a = jnp.dot(m, wup_ref[...], preferred_element_type=F32)
        a = jnp.square(jnp.maximum(a, 0.0)).astype(BF16)
        return jnp.dot(a, wdn_ref[...], preferred_element_type=F32)

    @pl.when(f == 0)
    def _():
        hs = [r[...] for r in h_refs]
        ms = [_rms(hq, g_ref[...]).astype(BF16) for hq in hs]
        for q in range(nparts):
            m_ref[q * tp:(q + 1) * tp, :] = ms[q]
        o_ref[...] = jnp.concatenate(hs, axis=0) + ffn(jnp.concatenate(ms, axis=0))

    @pl.when((f > 0) & (f < last))
    def _():
        o_ref[...] += ffn(m_ref[...])

    @pl.when(f == last)
    def _():
        for c in range(nchunks):
            rows = slice(c * rc, (c + 1) * rc)
            y = o_ref[rows, :] + ffn(m_ref[rows, :])
            o_ref[rows, :] = _rms(y, fg_ref[...])


def _mlp(h, g, wup, wdn, fg, tm, tf, nparts, nchunks):
    m, d = h.shape
    ff = wup.shape[1]
    nf = ff // tf
    assert nparts < nf
    tp = tm // nparts
    est = 2 * tm * d * 4 * 2 + tm * d * 2 + 2 * 2 * d * tf * 2 + tm * tf * 6 + tm * d * 4
    kern = functools.partial(_mlp_kernel, nparts=nparts, nchunks=nchunks)
    return pl.pallas_call(
        kern,
        grid=(m // tm, nf),
        in_specs=[pl.BlockSpec((tp, d), im) for im in _staggered_parts(m // tm, nparts)] + [
            pl.BlockSpec((1, d), lambda i, f: (0, 0)),
            pl.BlockSpec((d, tf), lambda i, f: (0, f)),
            pl.BlockSpec((tf, d), lambda i, f: (f, 0)),
            pl.BlockSpec((1, d), lambda i, f: (0, 0)),
        ],
        out_specs=pl.BlockSpec((tm, d), lambda i, f: (i, 0)),
        out_shape=jax.ShapeDtypeStruct((m, d), F32),
        scratch_shapes=[pltpu.VMEM((tm, d), BF16)],
        compiler_params=pltpu.CompilerParams(
            dimension_semantics=("arbitrary", "arbitrary"),
            vmem_limit_bytes=_vmem_limit(est + (8 << 20))),
        name="mlp",
    )(*([h] * nparts), g, wup, wdn, fg)


def _rope_tables(pos, dim):
    inv = 1.0 / (ROPE_BASE ** (jnp.arange(0, dim, 2, dtype=F32) / dim))
    ang = pos[:, None] * inv[None, :]
    return jnp.cos(ang), jnp.sin(ang)


def kernel(x, meta_tokens, norm_mix_g, w_in, ret_out_g, q_norm_g, w_uq, kv_norm_g,
           w_uk, w_uv, mla_out_g, w_o, norm_mlp_g, w_up, w_down, final_norm_g):
    B, S, D = x.shape
    assert norm_mix_g.shape[0] == 1 and meta_tokens.shape == (N_META, D)
    M = B * S
    xm = x.reshape(M, D)

    w_in0 = w_in[0].astype(BF16)
    w_in_aug = jnp.concatenate(
        [w_in0, w_in0[:, COL_KR:COL_KR + MLA_ROPE_DIM],
         jnp.zeros((D, PROJ_WIDTH - w_in0.shape[1] - MLA_ROPE_DIM), BF16)], axis=1)
    wq3 = w_uq[0].reshape(Q_LORA, MLA_HEADS, MLA_QK_DIM)
    wuq = jnp.concatenate(
        [wq3, jnp.zeros((Q_LORA, MLA_HEADS, HEAD_PAD - MLA_QK_DIM), wq3.dtype)],
        axis=2).reshape(Q_LORA, MLA_HEADS * HEAD_PAD).astype(BF16)
    wukt = w_uk[0].T.astype(BF16)
    sel = jnp.zeros((LANES, LANES), F32).at[:MLA_ROPE_DIM, :MLA_ROPE_DIM].set(
        jnp.eye(MLA_ROPE_DIM, dtype=F32)).astype(BF16)
    wuv = jnp.transpose(w_uv[0].reshape(KV_LORA, MLA_HEADS, MLA_V_DIM), (1, 0, 2)).astype(BF16)
    wo1 = w_o[0, :RET_WIDTH].astype(BF16)
    wo2 = w_o[0, RET_WIDTH:].astype(BF16)
    wup = w_up[0].astype(BF16)
    wdn = w_down[0].astype(BF16)

    pos = jnp.arange(S, dtype=F32) + float(N_META)
    posm = jnp.arange(N_META, dtype=F32)

    def ret_tables(p):
        c, s = _rope_tables(p, RET_HEAD_DIM)
        return jnp.concatenate([c, c], 1), jnp.concatenate([-s, s], 1)

    def mla_tables(p):
        c, s = _rope_tables(p, MLA_ROPE_DIM)
        return jnp.concatenate([c, c, c, c], 1), jnp.concatenate([-s, s, -s, s], 1)

    cos2, sin2 = ret_tables(pos)
    cosm, sinm = ret_tables(posm)
    pc, ps = mla_tables(pos)
    pcm, psm = mla_tables(posm)

    g_mix = norm_mix_g[0].reshape(1, D)
    proj = _inproj(xm, g_mix, w_in_aug, cos2, sin2, tm=1024, nparts=4, nchunks=2)
    proj_meta = _inproj(meta_tokens, g_mix, w_in_aug, cosm, sinm, tm=N_META, nparts=1, nchunks=1)
    projv = proj.reshape(B, S, PROJ_WIDTH)

    ret = _retention(projv, proj_meta, ret_out_g[0].reshape(1, RET_WIDTH), T=256)
    mla = _mla(projv, proj_meta, pc, ps, pcm, psm,
               q_norm_g[0].reshape(1, Q_LORA), kv_norm_g[0].reshape(1, KV_LORA),
               wuq, wukt, sel, wuv, mla_out_g[0].reshape(1, MLA_WIDTH), TQ=256, G=8)

    h = _oproj(xm, ret.reshape(M, RET_WIDTH), mla.reshape(M, MLA_WIDTH), wo1, wo2, tm=512)
    out = _mlp(h, norm_mlp_g[0].reshape(1, D), wup, wdn, final_norm_g.reshape(1, D),
               tm=1024, tf=512, nparts=4, nchunks=2)
    return out.reshape(B, S, D)
```

```python
import functools

import numpy as np
import jax
import jax.numpy as jnp
from jax import lax
from jax.experimental import pallas as pl
from jax.experimental.pallas import tpu as pltpu

F32 = jnp.float32
BF16 = jnp.bfloat16

CHUNK = 64
N_META = 16
RET_HEADS = 8
RET_HEAD_DIM = 128
MLA_HEADS = 8
MLA_V_DIM = 128
MLA_NOPE_DIM = 128
MLA_ROPE_DIM = 64
Q_LORA = 512
KV_LORA = 256
MLA_QK_DIM = MLA_NOPE_DIM + MLA_ROPE_DIM
ROPE_BASE = 10000.0
EPS = 1e-6
LOG2E = float(np.log2(np.e))

LANES = 128
RET_WIDTH = RET_HEADS * RET_HEAD_DIM
MLA_WIDTH = MLA_HEADS * MLA_V_DIM
COL_CQ = 4 * RET_WIDTH
COL_CKV = COL_CQ + Q_LORA
PROJ_WIDTH = 5120
KAUG = KV_LORA + LANES
HEAD_PAD = 256

VMEM_BYTES_V7X = 64 * 1024 * 1024
VMEM_LIMIT_CAP = VMEM_BYTES_V7X - 2 * 1024 * 1024


def _vmem_limit(nbytes):
    return int(min(VMEM_LIMIT_CAP, nbytes))


def _rms(x, g):
    ms = jnp.mean(x * x, axis=-1, keepdims=True)
    return x * lax.rsqrt(ms + EPS) * g


def _rope128(x, cos2, sin2):
    return x * cos2 + pltpu.roll(x, 64, 1) * sin2


def _staggered_parts(n_tiles, nparts):
    def make(q):
        def index_map(i, s):
            nxt = jnp.minimum(i + (s > q).astype(jnp.int32), n_tiles - 1)
            return (nparts * nxt + q, 0)
        return index_map
    return [make(q) for q in range(nparts)]


def _inproj_kernel(*refs, nparts, nchunks):
    x_refs = refs[:nparts]
    g_ref, w_ref, cos_ref, sin_ref, o_ref, u_ref = refs[nparts:]
    j = pl.program_id(1)
    tm = u_ref.shape[0]
    tp = tm // nparts
    rc = tm // nchunks

    def rope_heads(scale):
        def epilogue(acc, rows):
            cos2 = cos_ref[rows, :]
            sin2 = sin_ref[rows, :]
            return jnp.concatenate(
                [_rope128(acc[:, h * LANES:(h + 1) * LANES], cos2, sin2) * scale
                 for h in range(RET_HEADS)], axis=1)
        return epilogue

    def project(epilogue):
        for c in range(nchunks):
            rows = slice(c * rc, (c + 1) * rc)
            acc = jnp.dot(u_ref[rows, :], w_ref[...], preferred_element_type=F32)
            o_ref[rows, :] = epilogue(acc, rows).astype(o_ref.dtype)

    @pl.when(j == 0)
    def _():
        for q in range(nparts):
            u_ref[q * tp:(q + 1) * tp, :] = _rms(x_refs[q][...], g_ref[...]).astype(BF16)
        project(rope_heads(RET_HEAD_DIM ** -0.5))

    @pl.when(j == 1)
    def _():
        project(rope_heads(1.0))

    @pl.when(j == 3)
    def _():
        project(lambda acc, rows: acc * (1.0 / (1.0 + jnp.exp(-acc))))

    @pl.when((j == 2) | (j == 4))
    def _():
        o_ref[...] = jnp.dot(u_ref[...], w_ref[...],
                             preferred_element_type=F32).astype(o_ref.dtype)


def _inproj(x, g, w, cos2, sin2, tm, nparts, nchunks):
    m, d = x.shape
    n = w.shape[1]
    tn = RET_WIDTH
    assert n == 5 * tn and cos2.shape[0] % tm == 0 and nparts < n // tn
    seq_tiles = cos2.shape[0] // tm
    tp = tm // nparts
    est = (2 * tm * d * 4 + tm * d * 2 + 2 * d * tn * 2 + 2 * tm * tn * 2 + 2 * tm * tn * 4
           + tm * d * 4 + 4 * tm * LANES * 4)
    kern = functools.partial(_inproj_kernel, nparts=nparts, nchunks=nchunks)
    return pl.pallas_call(
        kern,
        grid=(m // tm, n // tn),
        in_specs=[pl.BlockSpec((tp, d), im) for im in _staggered_parts(m // tm, nparts)] + [
            pl.BlockSpec((1, d), lambda i, j: (0, 0)),
            pl.BlockSpec((d, tn), lambda i, j: (0, j)),
            pl.BlockSpec((tm, LANES), lambda i, j: (i % seq_tiles, 0)),
            pl.BlockSpec((tm, LANES), lambda i, j: (i % seq_tiles, 0)),
        ],
        out_specs=pl.BlockSpec((tm, tn), lambda i, j: (i, j)),
        out_shape=jax.ShapeDtypeStruct((m, n), BF16),
        scratch_shapes=[pltpu.VMEM((tm, d), BF16)],
        compiler_params=pltpu.CompilerParams(
            dimension_semantics=("arbitrary", "arbitrary"),
            vmem_limit_bytes=_vmem_limit(est + (8 << 20))),
        name="inproj",
    )(*([x] * nparts), g, w, cos2, sin2)


def _retention_kernel(q_ref, k_ref, v_ref, g_ref, km_ref, vm_ref, gout_ref,
                      o_ref,
                      state_ref, d_ref, wq_ref, wk_ref, *, T):
    b = pl.program_id(0)
    t = pl.program_id(1)
    H = RET_HEADS
    dk = RET_HEAD_DIM

    def log_gamma(h):
        e = jnp.full((1, 1), -5.0 - h, F32)
        return jnp.log(1.0 - jnp.exp2(e))

    @pl.when((b == 0) & (t == 0))
    def _():
        ii = lax.broadcasted_iota(jnp.int32, (T, T), 0)
        jj = lax.broadcasted_iota(jnp.int32, (T, T), 1)
        dist = jnp.abs(ii - jj).astype(F32)
        vis = (jj // CHUNK) <= (ii // CHUNK)
        row = lax.broadcasted_iota(jnp.int32, (T, LANES), 0).astype(F32)
        for h in range(H):
            lg = log_gamma(h)
            d_ref[h] = jnp.where(vis, jnp.exp(lg * dist), 0.0)
            wq_ref[h] = jnp.exp(lg * (row + 1.0))
            wk_ref[h] = jnp.exp(lg * (T - 1.0 - row))

    @pl.when(t == 0)
    def _():
        rowm = lax.broadcasted_iota(jnp.int32, (N_META, LANES), 0).astype(F32)
        for h in range(H):
            lg = log_gamma(h)
            km = km_ref[:, h * dk:(h + 1) * dk].astype(F32) * jnp.exp(lg * (N_META - 1.0 - rowm))
            vm = vm_ref[:, h * dk:(h + 1) * dk]
            state_ref[h] = lax.dot_general(
                km.astype(BF16), vm, (((0,), (0,)), ((), ())),
                preferred_element_type=F32)

    pre = []
    for h in range(H):
        cs = slice(h * dk, (h + 1) * dk)
        lg = log_gamma(h)
        q = q_ref[0, :, cs]
        k = k_ref[0, :, cs]
        v = v_ref[0, :, cs]
        s = lax.dot_general(q, k, (((1,), (1,)), ((), ())), preferred_element_type=F32)
        st = state_ref[h]
        inter = jnp.dot((q.astype(F32) * wq_ref[h]).astype(BF16), st.astype(BF16),
                        preferred_element_type=F32)
        kv = lax.dot_general((k.astype(F32) * wk_ref[h]).astype(BF16), v,
                             (((0,), (0,)), ((), ())), preferred_element_type=F32)
        state_ref[h] = st * jnp.exp(lg * float(T)) + kv
        pre.append((s, inter))
    for h in range(H):
        cs = slice(h * dk, (h + 1) * dk)
        s, inter = pre[h]
        sd = (s * d_ref[h]).astype(BF16)
        o = jnp.dot(sd, v_ref[0, :, cs], preferred_element_type=F32) + inter
        mu = jnp.mean(o, axis=-1, keepdims=True)
        oc = o - mu
        var = jnp.mean(oc * oc, axis=-1, keepdims=True)
        y = oc * lax.rsqrt(var + EPS) * gout_ref[:, cs]
        o_ref[0, :, cs] = (g_ref[0, :, cs].astype(F32) * y).astype(o_ref.dtype)


def _retention(projv, proj_meta, gout, T):
    B, S, _ = projv.shape
    W = RET_WIDTH
    est = (2 * 4 * T * W * 2 + 2 * T * W * 2
           + RET_HEADS * (T * T + 2 * T * LANES + 128 * 128) * 4 + 16 * T * T * 4)
    kern = functools.partial(_retention_kernel, T=T)
    return pl.pallas_call(
        kern,
        grid=(B, S // T),
        in_specs=[
            pl.BlockSpec((1, T, W), lambda b, t: (b, t, 0)),
            pl.BlockSpec((1, T, W), lambda b, t: (b, t, 1)),
            pl.BlockSpec((1, T, W), lambda b, t: (b, t, 2)),
            pl.BlockSpec((1, T, W), lambda b, t: (b, t, 3)),
            pl.BlockSpec((N_META, W), lambda b, t: (0, 1)),
            pl.BlockSpec((N_META, W), lambda b, t: (0, 2)),
            pl.BlockSpec((1, W), lambda b, t: (0, 0)),
        ],
        out_specs=pl.BlockSpec((1, T, W), lambda b, t: (b, t, 0)),
        out_shape=jax.ShapeDtypeStruct((B, S, W), BF16),
        scratch_shapes=[
            pltpu.VMEM((RET_HEADS, RET_HEAD_DIM, RET_HEAD_DIM), F32),
            pltpu.VMEM((RET_HEADS, T, T), F32),
            pltpu.VMEM((RET_HEADS, T, LANES), F32),
            pltpu.VMEM((RET_HEADS, T, LANES), F32),
        ],
        compiler_params=pltpu.CompilerParams(
            dimension_semantics=("arbitrary", "arbitrary"),
            vmem_limit_bytes=_vmem_limit(est + (8 << 20))),
        name="retention",
    )(projv, projv, projv, projv, proj_meta, proj_meta, gout)


def _swap32(x):
    lane = lax.broadcasted_iota(jnp.int32, x.shape, 1)
    return jnp.where((lane % 64) < 32, pltpu.roll(x, 96, 1), pltpu.roll(x, 32, 1))


def _rope64x2(x, pc, ps):
    return x * pc + _swap32(x) * ps


def _mla_kernel(cq_ref, kv_ref, kvm_ref, pc_ref, ps_ref, pcm_ref, psm_ref,
                qg_ref, kvg_ref, wuq_ref, wukt_ref, sel_ref, wuv_ref, og_ref,
                o_ref,
                kaug_ref, kt_ref, qh_ref, p_ref, m_ref, l_ref, a_ref, acc_ref,
                *, TQ, TK, G):
    i = pl.program_id(1)
    H = MLA_HEADS
    NU = kt_ref.shape[0]
    qscale = (MLA_QK_DIM ** -0.5) * LOG2E
    NEG = -1e30

    def prep_keys(kv, pc, ps):
        c = _rms(kv[:, :KV_LORA], kvg_ref[...])
        kr = _rope64x2(kv[:, KV_LORA:], pc, ps)
        return c.astype(BF16), kr.astype(BF16)

    @pl.when(i == 0)
    def _():
        kaug_ref[0:TK, :] = jnp.zeros((TK, KAUG), BF16)
        cm, krm = prep_keys(kvm_ref[...].astype(F32), pcm_ref[...], psm_ref[...])
        kaug_ref[0:N_META, 0:KV_LORA] = cm
        kaug_ref[0:N_META, KV_LORA:] = krm
        c, kr = prep_keys(kv_ref[0].astype(F32), pc_ref[...], ps_ref[...])
        kaug_ref[TK:, 0:KV_LORA] = c
        kaug_ref[TK:, KV_LORA:] = kr
        nt = (((1,), (1,)), ((), ()))
        for u in range(NU):
            ku = kaug_ref[u * TK:(u + 1) * TK, :]
            kn = lax.dot_general(wukt_ref[...], ku[:, 0:KV_LORA], nt,
                                 preferred_element_type=F32).astype(BF16)
            krt = lax.dot_general(sel_ref[...], ku[:, KV_LORA:], nt,
                                  preferred_element_type=F32).astype(BF16)
            for h in range(H):
                kt_ref[u, h * HEAD_PAD:h * HEAD_PAD + LANES, :] = kn[h * LANES:(h + 1) * LANES]
                kt_ref[u, h * HEAD_PAD + LANES:(h + 1) * HEAD_PAD, :] = krt

    q0 = pl.multiple_of(i * TQ, TQ)
    cq = _rms(cq_ref[0].astype(F32), qg_ref[...]).astype(BF16)
    q = jnp.dot(cq, wuq_ref[...], preferred_element_type=F32)
    pcq = pc_ref[pl.ds(q0, TQ), :]
    psq = ps_ref[pl.ds(q0, TQ), :]
    for h in range(H):
        c0 = h * HEAD_PAD
        qh_ref[h * TQ:(h + 1) * TQ, 0:LANES] = (q[:, c0:c0 + LANES] * qscale).astype(BF16)
        rp = _rope64x2(q[:, c0 + LANES:c0 + HEAD_PAD], pcq, psq) * qscale
        qh_ref[h * TQ:(h + 1) * TQ, LANES:] = rp.astype(BF16)

    def block(units, bias, first):
        W = sum(w for _, w in units)
        def key_rows(u, w):
            if isinstance(u, int):
                return kaug_ref[u * TK:u * TK + w, 0:KV_LORA]
            return kaug_ref[pl.ds(pl.multiple_of(u * TK, TK), w), 0:KV_LORA]

        vb = jnp.concatenate([key_rows(u, w) for u, w in units], axis=0)
        HG = H // G
        AHEAD = 2

        def scores(g):
            out = []
            for h in range(g * HG, (g + 1) * HG):
                qh = qh_ref[h * TQ:(h + 1) * TQ, :]
                out.append([jnp.dot(qh, kt_ref[u, h * HEAD_PAD:(h + 1) * HEAD_PAD, 0:w],
                                    preferred_element_type=F32) for u, w in units])
            return out

        s_groups = {g: scores(g) for g in range(min(AHEAD, G))}
        for g in range(G):
            gs = slice(g * HG * TQ, (g + 1) * HG * TQ)
            s_g = s_groups.pop(g)
            for hh in range(HG):
                h = g * HG + hh
                rs = slice(h * TQ, (h + 1) * TQ)
                cols = []
                for ui, s in enumerate(s_g[hh]):
                    if bias is not None and bias[ui] is not None:
                        s = s + bias[ui]
                    cols += [s[:, c * LANES:(c + 1) * LANES] for c in range(s.shape[1] // LANES)]
                m_cur = jnp.max(functools.reduce(jnp.maximum, cols), axis=-1, keepdims=True)
                if first:
                    m_new = jnp.broadcast_to(m_cur, (TQ, LANES))
                else:
                    m_prev = m_ref[rs]
                    m_new = jnp.maximum(m_prev, m_cur)
                    alpha = jnp.exp2(m_prev - m_new)
                    a_ref[rs] = alpha
                ps = [jnp.exp2(c - m_new) for c in cols]
                psum = functools.reduce(lambda x, y: x + y, ps)
                if first:
                    l_ref[rs] = psum
                else:
                    l_ref[rs] = alpha * l_ref[rs] + psum
                m_ref[rs] = m_new
                for c, pc_ in enumerate(ps):
                    p_ref[rs, c * LANES:(c + 1) * LANES] = pc_.astype(BF16)
            o = jnp.dot(p_ref[gs, 0:W], vb, preferred_element_type=F32)
            if first:
                acc_ref[gs, :] = o
            else:
                a = a_ref[gs, :]
                acc_ref[gs, 0:LANES] = a * acc_ref[gs, 0:LANES] + o[:, 0:LANES]
                acc_ref[gs, LANES:] = a * acc_ref[gs, LANES:] + o[:, LANES:]
            if g + AHEAD < G:
                s_groups[g + AHEAD] = scores(g + AHEAD)

    colh = lax.broadcasted_iota(jnp.int32, (TQ, TK), 1)
    bias_h = jnp.where(lax.broadcasted_iota(jnp.int32, (TQ, LANES), 1) < N_META, 0.0, NEG).astype(F32)
    rq = lax.broadcasted_iota(jnp.int32, (TQ, TK), 0) // CHUNK
    bias_d = jnp.where(colh // CHUNK <= rq, 0.0, NEG).astype(F32)

    block([(0, LANES), (i + 1, TK)], [bias_h, bias_d], True)

    def body(t, carry):
        block([(2 * t + 1, TK), (2 * t + 2, TK)], None, False)
        return carry

    lax.fori_loop(0, i // 2, body, 0)

    @pl.when(i % 2 == 1)
    def _():
        block([(i, TK)], None, False)

    outs = []
    for h in range(H):
        rs = slice(h * TQ, (h + 1) * TQ)
        inv_l = 1.0 / jnp.sum(l_ref[rs], axis=-1, keepdims=True)
        o = (acc_ref[rs] * inv_l).astype(BF16)
        outs.append(jnp.dot(o, wuv_ref[h], preferred_element_type=F32))
    mla = jnp.concatenate(outs, axis=-1)
    o_ref[0] = _rms(mla, og_ref[...]).astype(o_ref.dtype)


def _mla(projv, proj_meta, pc, ps, pcm, psm, qg, kvg, wuq, wukt, sel, wuv, og, TQ, G):
    B, S, _ = projv.shape
    TK = TQ
    NU = 1 + S // TK
    R = MLA_HEADS * TQ
    nq = wuq.shape[1]
    HP = MLA_HEADS * HEAD_PAD
    est = (2 * TQ * Q_LORA * 2 + 2 * S * KAUG * 2 + 4 * S * LANES * 4
           + 2 * (Q_LORA * nq + 2 * MLA_HEADS * 128 * KV_LORA + LANES * LANES) * 2 + 2 * TQ * MLA_WIDTH * 2
           + NU * TK * KAUG * 2 + NU * HP * TK * 2 + R * HEAD_PAD * 2 + R * 2 * TK * 2
           + 3 * R * LANES * 4 + R * KV_LORA * 4
           + 3 * R * TK * 4 + S * KAUG * 4 * 2)
    kern = functools.partial(_mla_kernel, TQ=TQ, TK=TK, G=G)
    const2 = lambda b, i: (0, 0)
    const3 = lambda b, i: (0, 0, 0)
    return pl.pallas_call(
        kern,
        grid=(B, S // TQ),
        in_specs=[
            pl.BlockSpec((1, TQ, Q_LORA), lambda b, i: (b, i, COL_CQ // Q_LORA)),
            pl.BlockSpec((1, S, KAUG), lambda b, i: (b, 0, COL_CKV // KAUG)),
            pl.BlockSpec((N_META, KAUG), lambda b, i: (0, COL_CKV // KAUG)),
            pl.BlockSpec((S, LANES), const2),
            pl.BlockSpec((S, LANES), const2),
            pl.BlockSpec((N_META, LANES), const2),
            pl.BlockSpec((N_META, LANES), const2),
            pl.BlockSpec((1, Q_LORA), const2),
            pl.BlockSpec((1, KV_LORA), const2),
            pl.BlockSpec((Q_LORA, nq), const2),
            pl.BlockSpec((MLA_HEADS * MLA_NOPE_DIM, KV_LORA), const2),
            pl.BlockSpec((LANES, LANES), const2),
            pl.BlockSpec((MLA_HEADS, KV_LORA, MLA_V_DIM), const3),
            pl.BlockSpec((1, MLA_WIDTH), const2),
        ],
        out_specs=pl.BlockSpec((1, TQ, MLA_WIDTH), lambda b, i: (b, i, 0)),
        out_shape=jax.ShapeDtypeStruct((B, S, MLA_WIDTH), BF16),
        scratch_shapes=[
            pltpu.VMEM((NU * TK, KAUG), BF16),
            pltpu.VMEM((NU, HP, TK), BF16),
            pltpu.VMEM((R, HEAD_PAD), BF16),
            pltpu.VMEM((R, 2 * TK), BF16),
            pltpu.VMEM((R, LANES), F32),
            pltpu.VMEM((R, LANES), F32),
            pltpu.VMEM((R, LANES), F32),
            pltpu.VMEM((R, KV_LORA), F32),
        ],
        compiler_params=pltpu.CompilerParams(
            dimension_semantics=("arbitrary", "arbitrary"),
            vmem_limit_bytes=_vmem_limit(est + (8 << 20))),
        name="mla",
    )(projv, projv, proj_meta, pc, ps, pcm, psm, qg, kvg, wuq, wukt, sel, wuv, og)


def _oproj_kernel(x_ref, a1_ref, a2_ref, w1_ref, w2_ref, o_ref):
    acc = jnp.dot(a1_ref[...], w1_ref[...], preferred_element_type=F32)
    acc = acc + jnp.dot(a2_ref[...], w2_ref[...], preferred_element_type=F32)
    o_ref[...] = x_ref[...] + acc


def _oproj(x, a1, a2, w1, w2, tm):
    m, d = x.shape
    k = a1.shape[1]
    est = 2 * tm * d * 4 * 2 + 2 * 2 * tm * k * 2 + 2 * 2 * k * d * 2 + tm * d * 4
    return pl.pallas_call(
        _oproj_kernel,
        grid=(m // tm,),
        in_specs=[
            pl.BlockSpec((tm, d), lambda i: (i, 0)),
            pl.BlockSpec((tm, k), lambda i: (i, 0)),
            pl.BlockSpec((tm, k), lambda i: (i, 0)),
            pl.BlockSpec((k, d), lambda i: (0, 0)),
            pl.BlockSpec((k, d), lambda i: (0, 0)),
        ],
        out_specs=pl.BlockSpec((tm, d), lambda i: (i, 0)),
        out_shape=jax.ShapeDtypeStruct((m, d), F32),
        compiler_params=pltpu.CompilerParams(
            dimension_semantics=("arbitrary",),
            vmem_limit_bytes=_vmem_limit(est + (8 << 20))),
        name="oproj",
    )(x, a1, a2, w1, w2)


def _mlp_kernel(h_ref, g_ref, wup_ref, wdn_ref, fg_ref, o_ref, m_ref, *, nchunks, nsplit):
    f = pl.program_id(1)
    last = pl.num_programs(1) - 1
    rc = m_ref.shape[0] // nchunks

    def ffn(m):
        fs = wup_ref.shape[1] // nsplit
        ups = [jnp.dot(m, wup_ref[:, s * fs:(s + 1) * fs], preferred_element_type=F32)
               for s in range(nsplit)]
        out = None
        for s in range(nsplit):
            a = jnp.square(jnp.maximum(ups[s], 0.0)).astype(BF16)
            d = jnp.dot(a, wdn_ref[s * fs:(s + 1) * fs, :], preferred_element_type=F32)
            out = d if out is None else out + d
        return out

    @pl.when(f == 0)
    def _():
        m = _rms(h_ref[...], g_ref[...]).astype(BF16)
        m_ref[...] = m
        o_ref[...] = ffn(m)

    @pl.when((f > 0) & (f < last))
    def _():
        o_ref[...] += ffn(m_ref[...])

    @pl.when(f == last)
    def _():
        for c in range(nchunks):
            rows = slice(c * rc, (c + 1) * rc)
            y = o_ref[rows, :] + ffn(m_ref[rows, :]) + h_ref[rows, :]
            o_ref[rows, :] = _rms(y, fg_ref[...])


def _mlp(h, g, wup, wdn, fg, tm, tf, nchunks, nsplit):
    m, d = h.shape
    ff = wup.shape[1]
    est = 2 * tm * d * 4 * 2 + tm * d * 2 + 2 * 2 * d * tf * 2 + tm * tf * 6
    kern = functools.partial(_mlp_kernel, nchunks=nchunks, nsplit=nsplit)
    return pl.pallas_call(
        kern,
        grid=(m // tm, ff // tf),
        in_specs=[
            pl.BlockSpec((tm, d), lambda i, f: (i, 0)),
            pl.BlockSpec((1, d), lambda i, f: (0, 0)),
            pl.BlockSpec((d, tf), lambda i, f: (0, f)),
            pl.BlockSpec((tf, d), lambda i, f: (f, 0)),
            pl.BlockSpec((1, d), lambda i, f: (0, 0)),
        ],
        out_specs=pl.BlockSpec((tm, d), lambda i, f: (i, 0)),
        out_shape=jax.ShapeDtypeStruct((m, d), F32),
        scratch_shapes=[pltpu.VMEM((tm, d), BF16)],
        compiler_params=pltpu.CompilerParams(
            dimension_semantics=("arbitrary", "arbitrary"),
            vmem_limit_bytes=_vmem_limit(est + (8 << 20))),
        name="mlp",
    )(h, g, wup, wdn, fg)


def _rope_tables(pos, dim):
    inv = 1.0 / (ROPE_BASE ** (jnp.arange(0, dim, 2, dtype=F32) / dim))
    ang = pos[:, None] * inv[None, :]
    return jnp.cos(ang), jnp.sin(ang)


def kernel(x, meta_tokens, norm_mix_g, w_in, ret_out_g, q_norm_g, w_uq, kv_norm_g,
           w_uk, w_uv, mla_out_g, w_o, norm_mlp_g, w_up, w_down, final_norm_g):
    B, S, D = x.shape
    assert norm_mix_g.shape[0] == 1 and meta_tokens.shape == (N_META, D)
    M = B * S
    xm = x.reshape(M, D)

    w_in_aug = jnp.pad(w_in[0].astype(BF16), ((0, 0), (0, PROJ_WIDTH - w_in.shape[2])))
    wq3 = w_uq[0].reshape(Q_LORA, MLA_HEADS, MLA_QK_DIM)
    wuq = jnp.concatenate(
        [wq3, jnp.zeros((Q_LORA, MLA_HEADS, HEAD_PAD - MLA_QK_DIM), wq3.dtype)],
        axis=2).reshape(Q_LORA, MLA_HEADS * HEAD_PAD).astype(BF16)
    wukt = w_uk[0].T.astype(BF16)
    sel = jnp.zeros((LANES, LANES), F32).at[:MLA_ROPE_DIM, :MLA_ROPE_DIM].set(
        jnp.eye(MLA_ROPE_DIM, dtype=F32)).astype(BF16)
    wuv = jnp.transpose(w_uv[0].reshape(KV_LORA, MLA_HEADS, MLA_V_DIM), (1, 0, 2)).astype(BF16)
    wo1 = w_o[0, :RET_WIDTH].astype(BF16)
    wo2 = w_o[0, RET_WIDTH:].astype(BF16)
    wup = w_up[0].astype(BF16)
    wdn = w_down[0].astype(BF16)

    pos = jnp.arange(S, dtype=F32) + float(N_META)
    posm = jnp.arange(N_META, dtype=F32)

    def ret_tables(p):
        c, s = _rope_tables(p, RET_HEAD_DIM)
        return jnp.concatenate([c, c], 1), jnp.concatenate([-s, s], 1)

    def mla_tables(p):
        c, s = _rope_tables(p, MLA_ROPE_DIM)
        return jnp.concatenate([c, c, c, c], 1), jnp.concatenate([-s, s, -s, s], 1)

    cos2, sin2 = ret_tables(pos)
    cosm, sinm = ret_tables(posm)
    pc, ps = mla_tables(pos)
    pcm, psm = mla_tables(posm)

    g_mix = norm_mix_g[0].reshape(1, D)
    proj = _inproj(xm, g_mix, w_in_aug, cos2, sin2, tm=1024, nparts=4, nchunks=2)
    proj_meta = _inproj(meta_tokens, g_mix, w_in_aug, cosm, sinm, tm=N_META, nparts=1, nchunks=1)
    projv = proj.reshape(B, S, PROJ_WIDTH)

    ret = _retention(projv, proj_meta, ret_out_g[0].reshape(1, RET_WIDTH), T=256)
    mla = _mla(projv, proj_meta, pc, ps, pcm, psm,
               q_norm_g[0].reshape(1, Q_LORA), kv_norm_g[0].reshape(1, KV_LORA),
               wuq, wukt, sel, wuv, mla_out_g[0].reshape(1, MLA_WIDTH), TQ=256, G=8)

    h = _oproj(xm, ret.reshape(M, RET_WIDTH), mla.reshape(M, MLA_WIDTH), wo1, wo2, tm=512)
    out = _mlp(h, norm_mlp_g[0].reshape(1, D), wup, wdn, final_norm_g.reshape(1, D),
               tm=1024, tf=1024, nchunks=2, nsplit=2)
    return out.reshape(B, S, D)
```

```python
import functools

import numpy as np
import jax
import jax.numpy as jnp
from jax import lax
from jax.experimental import pallas as pl
from jax.experimental.pallas import tpu as pltpu

F32 = jnp.float32
BF16 = jnp.bfloat16

CHUNK = 64
N_META = 16
RET_HEADS = 8
RET_HEAD_DIM = 128
MLA_HEADS = 8
MLA_V_DIM = 128
MLA_NOPE_DIM = 128
MLA_ROPE_DIM = 64
Q_LORA = 512
KV_LORA = 256
MLA_QK_DIM = MLA_NOPE_DIM + MLA_ROPE_DIM
ROPE_BASE = 10000.0
EPS = 1e-6
LOG2E = float(np.log2(np.e))

LANES = 128
RET_WIDTH = RET_HEADS * RET_HEAD_DIM
MLA_WIDTH = MLA_HEADS * MLA_V_DIM
COL_CQ = 4 * RET_WIDTH
COL_CKV = COL_CQ + Q_LORA
PROJ_WIDTH = 5120
KAUG = KV_LORA + LANES
HEAD_PAD = 256

VMEM_BYTES_V7X = 64 * 1024 * 1024
VMEM_LIMIT_CAP = VMEM_BYTES_V7X - 2 * 1024 * 1024


def _vmem_limit(nbytes):
    return int(min(VMEM_LIMIT_CAP, nbytes))


def _rms(x, g):
    ms = jnp.mean(x * x, axis=-1, keepdims=True)
    return x * lax.rsqrt(ms + EPS) * g


def _rope128(x, cos2, sin2):
    return x * cos2 + pltpu.roll(x, 64, 1) * sin2


def _inproj_kernel(x_ref, g_ref, w_ref, cos_ref, sin_ref, o_ref):
    tn = RET_WIDTH
    u = _rms(x_ref[...], g_ref[...]).astype(BF16)
    cos2 = cos_ref[...]
    sin2 = sin_ref[...]

    def rope_heads(acc, scale):
        return jnp.concatenate(
            [_rope128(acc[:, h * LANES:(h + 1) * LANES], cos2, sin2) * scale
             for h in range(RET_HEADS)], axis=1)

    epilogues = {
        0: lambda acc: rope_heads(acc, RET_HEAD_DIM ** -0.5),
        1: lambda acc: rope_heads(acc, 1.0),
        3: lambda acc: acc * (1.0 / (1.0 + jnp.exp(-acc))),
    }
    for j in range(w_ref.shape[1] // tn):
        acc = jnp.dot(u, w_ref[:, j * tn:(j + 1) * tn], preferred_element_type=F32)
        if j in epilogues:
            acc = epilogues[j](acc)
        o_ref[:, j * tn:(j + 1) * tn] = acc.astype(o_ref.dtype)


def _inproj(x, g, w, cos2, sin2, tm):
    m, d = x.shape
    n = w.shape[1]
    assert n == 5 * RET_WIDTH and cos2.shape[0] % tm == 0
    seq_tiles = cos2.shape[0] // tm
    est = (2 * tm * d * 4 + tm * d * 2 + d * n * 2 + 2 * tm * n * 2 + 3 * tm * RET_WIDTH * 4
           + tm * d * 4 + 4 * tm * LANES * 4)
    return pl.pallas_call(
        _inproj_kernel,
        grid=(m // tm,),
        in_specs=[
            pl.BlockSpec((tm, d), lambda i: (i, 0)),
            pl.BlockSpec((1, d), lambda i: (0, 0)),
            pl.BlockSpec((d, n), lambda i: (0, 0), pipeline_mode=pl.Buffered(1)),
            pl.BlockSpec((tm, LANES), lambda i: (i % seq_tiles, 0)),
            pl.BlockSpec((tm, LANES), lambda i: (i % seq_tiles, 0)),
        ],
        out_specs=pl.BlockSpec((tm, n), lambda i: (i, 0)),
        out_shape=jax.ShapeDtypeStruct((m, n), BF16),
        compiler_params=pltpu.CompilerParams(
            dimension_semantics=("arbitrary",),
            vmem_limit_bytes=_vmem_limit(est + (8 << 20))),
        name="inproj",
    )(x, g, w, cos2, sin2)


def _retention_kernel(q_ref, k_ref, v_ref, g_ref, km_ref, vm_ref, gout_ref,
                      o_ref,
                      state_ref, d_ref, wq_ref, wk_ref, *, T):
    b = pl.program_id(0)
    t = pl.program_id(1)
    H = RET_HEADS
    dk = RET_HEAD_DIM

    def log_gamma(h):
        e = jnp.full((1, 1), -5.0 - h, F32)
        return jnp.log(1.0 - jnp.exp2(e))

    @pl.when((b == 0) & (t == 0))
    def _():
        ii = lax.broadcasted_iota(jnp.int32, (T, T), 0)
        jj = lax.broadcasted_iota(jnp.int32, (T, T), 1)
        dist = jnp.abs(ii - jj).astype(F32)
        vis = (jj // CHUNK) <= (ii // CHUNK)
        row = lax.broadcasted_iota(jnp.int32, (T, LANES), 0).astype(F32)
        for h in range(H):
            lg = log_gamma(h)
            d_ref[h] = jnp.where(vis, jnp.exp(lg * dist), 0.0)
            wq_ref[h] = jnp.exp(lg * (row + 1.0))
            wk_ref[h] = jnp.exp(lg * (T - 1.0 - row))

    @pl.when(t == 0)
    def _():
        rowm = lax.broadcasted_iota(jnp.int32, (N_META, LANES), 0).astype(F32)
        for h in range(H):
            lg = log_gamma(h)
            km = km_ref[:, h * dk:(h + 1) * dk].astype(F32) * jnp.exp(lg * (N_META - 1.0 - rowm))
            vm = vm_ref[:, h * dk:(h + 1) * dk]
            state_ref[h] = lax.dot_general(
                km.astype(BF16), vm, (((0,), (0,)), ((), ())),
                preferred_element_type=F32)

    pre = []
    for h in range(H):
        cs = slice(h * dk, (h + 1) * dk)
        lg = log_gamma(h)
        q = q_ref[0, :, cs]
        k = k_ref[0, :, cs]
        v = v_ref[0, :, cs]
        s = lax.dot_general(q, k, (((1,), (1,)), ((), ())), preferred_element_type=F32)
        st = state_ref[h]
        inter = jnp.dot((q.astype(F32) * wq_ref[h]).astype(BF16), st.astype(BF16),
                        preferred_element_type=F32)
        kv = lax.dot_general((k.astype(F32) * wk_ref[h]).astype(BF16), v,
                             (((0,), (0,)), ((), ())), preferred_element_type=F32)
        state_ref[h] = st * jnp.exp(lg * float(T)) + kv
        pre.append((s, inter))
    for h in range(H):
        cs = slice(h * dk, (h + 1) * dk)
        s, inter = pre[h]
        sd = (s * d_ref[h]).astype(BF16)
        o = jnp.dot(sd, v_ref[0, :, cs], preferred_element_type=F32) + inter
        mu = jnp.mean(o, axis=-1, keepdims=True)
        oc = o - mu
        var = jnp.mean(oc * oc, axis=-1, keepdims=True)
        y = oc * lax.rsqrt(var + EPS) * gout_ref[:, cs]
        o_ref[0, :, cs] = (g_ref[0, :, cs].astype(F32) * y).astype(o_ref.dtype)


def _retention(projv, proj_meta, gout, T):
    B, S, _ = projv.shape
    W = RET_WIDTH
    est = (2 * 4 * T * W * 2 + 2 * T * W * 2
           + RET_HEADS * (T * T + 2 * T * LANES + 128 * 128) * 4 + 16 * T * T * 4)
    kern = functools.partial(_retention_kernel, T=T)
    return pl.pallas_call(
        kern,
        grid=(B, S // T),
        in_specs=[
            pl.BlockSpec((1, T, W), lambda b, t: (b, t, 0)),
            pl.BlockSpec((1, T, W), lambda b, t: (b, t, 1)),
            pl.BlockSpec((1, T, W), lambda b, t: (b, t, 2)),
            pl.BlockSpec((1, T, W), lambda b, t: (b, t, 3)),
            pl.BlockSpec((N_META, W), lambda b, t: (0, 1)),
            pl.BlockSpec((N_META, W), lambda b, t: (0, 2)),
            pl.BlockSpec((1, W), lambda b, t: (0, 0)),
        ],
        out_specs=pl.BlockSpec((1, T, W), lambda b, t: (b, t, 0)),
        out_shape=jax.ShapeDtypeStruct((B, S, W), BF16),
        scratch_shapes=[
            pltpu.VMEM((RET_HEADS, RET_HEAD_DIM, RET_HEAD_DIM), F32),
            pltpu.VMEM((RET_HEADS, T, T), F32),
            pltpu.VMEM((RET_HEADS, T, LANES), F32),
            pltpu.VMEM((RET_HEADS, T, LANES), F32),
        ],
        compiler_params=pltpu.CompilerParams(
            dimension_semantics=("arbitrary", "arbitrary"),
            vmem_limit_bytes=_vmem_limit(est + (8 << 20))),
        name="retention",
    )(projv, projv, projv, projv, proj_meta, proj_meta, gout)


def _swap32(x):
    lane = lax.broadcasted_iota(jnp.int32, x.shape, 1)
    return jnp.where((lane % 64) < 32, pltpu.roll(x, 96, 1), pltpu.roll(x, 32, 1))


def _rope64x2(x, pc, ps):
    return x * pc + _swap32(x) * ps


def _mla_kernel(cq_ref, kv_ref, kvm_ref, pc_ref, ps_ref, pcm_ref, psm_ref,
                qg_ref, kvg_ref, wuq_ref, wukt_ref, sel_ref, wuv_ref, og_ref,
                o_ref,
                kaug_ref, kt_ref, qh_ref, p_ref, m_ref, l_ref, a_ref, acc_ref,
                *, TQ, TK, G):
    i = pl.program_id(1)
    H = MLA_HEADS
    NU = kt_ref.shape[0]
    qscale = (MLA_QK_DIM ** -0.5) * LOG2E
    NEG = -1e30

    def prep_keys(kv, pc, ps):
        c = _rms(kv[:, :KV_LORA], kvg_ref[...])
        kr = _rope64x2(kv[:, KV_LORA:], pc, ps)
        return c.astype(BF16), kr.astype(BF16)

    @pl.when(i == 0)
    def _():
        kaug_ref[0:TK, :] = jnp.zeros((TK, KAUG), BF16)
        cm, krm = prep_keys(kvm_ref[...].astype(F32), pcm_ref[...], psm_ref[...])
        kaug_ref[0:N_META, 0:KV_LORA] = cm
        kaug_ref[0:N_META, KV_LORA:] = krm
        c, kr = prep_keys(kv_ref[0].astype(F32), pc_ref[...], ps_ref[...])
        kaug_ref[TK:, 0:KV_LORA] = c
        kaug_ref[TK:, KV_LORA:] = kr
        nt = (((1,), (1,)), ((), ()))
        for u in range(NU):
            ku = kaug_ref[u * TK:(u + 1) * TK, :]
            kn = lax.dot_general(wukt_ref[...], ku[:, 0:KV_LORA], nt,
                                 preferred_element_type=F32).astype(BF16)
            krt = lax.dot_general(sel_ref[...], ku[:, KV_LORA:], nt,
                                  preferred_element_type=F32).astype(BF16)
            for h in range(H):
                kt_ref[u, h * HEAD_PAD:h * HEAD_PAD + LANES, :] = kn[h * LANES:(h + 1) * LANES]
                kt_ref[u, h * HEAD_PAD + LANES:(h + 1) * HEAD_PAD, :] = krt

        QC = 2 * TQ
        for c in range(cq_ref.shape[1] // QC):
            rows = slice(c * QC, (c + 1) * QC)
            cq = _rms(cq_ref[0, rows, :].astype(F32), qg_ref[...]).astype(BF16)
            q = jnp.dot(cq, wuq_ref[...], preferred_element_type=F32)
            pcq = pc_ref[rows, :]
            psq = ps_ref[rows, :]
            for h in range(H):
                c0 = h * HEAD_PAD
                qn = (q[:, c0:c0 + LANES] * qscale).astype(BF16)
                rp = (_rope64x2(q[:, c0 + LANES:c0 + HEAD_PAD], pcq, psq) * qscale).astype(BF16)
                for sb in range(QC // TQ):
                    qb = c * (QC // TQ) + sb
                    qh_ref[qb, h * TQ:(h + 1) * TQ, 0:LANES] = qn[sb * TQ:(sb + 1) * TQ]
                    qh_ref[qb, h * TQ:(h + 1) * TQ, LANES:] = rp[sb * TQ:(sb + 1) * TQ]

    def block(units, bias, first):
        W = sum(w for _, w in units)
        def key_rows(u, w):
            if isinstance(u, int):
                return kaug_ref[u * TK:u * TK + w, 0:KV_LORA]
            return kaug_ref[pl.ds(pl.multiple_of(u * TK, TK), w), 0:KV_LORA]

        vb = jnp.concatenate([key_rows(u, w) for u, w in units], axis=0)
        HG = H // G
        AHEAD = 2

        def scores(g):
            out = []
            for h in range(g * HG, (g + 1) * HG):
                qh = qh_ref[i, h * TQ:(h + 1) * TQ, :]
                out.append([jnp.dot(qh, kt_ref[u, h * HEAD_PAD:(h + 1) * HEAD_PAD, 0:w],
                                    preferred_element_type=F32) for u, w in units])
            return out

        s_groups = {g: scores(g) for g in range(min(AHEAD, G))}
        for g in range(G):
            gs = slice(g * HG * TQ, (g + 1) * HG * TQ)
            s_g = s_groups.pop(g)
            for hh in range(HG):
                h = g * HG + hh
                rs = slice(h * TQ, (h + 1) * TQ)
                cols = []
                for ui, s in enumerate(s_g[hh]):
                    if bias is not None and bias[ui] is not None:
                        s = s + bias[ui]
                    cols += [s[:, c * LANES:(c + 1) * LANES] for c in range(s.shape[1] // LANES)]
                m_cur = jnp.max(functools.reduce(jnp.maximum, cols), axis=-1, keepdims=True)
                if first:
                    m_new = jnp.broadcast_to(m_cur, (TQ, LANES))
                else:
                    m_prev = m_ref[rs]
                    m_new = jnp.maximum(m_prev, m_cur)
                    alpha = jnp.exp2(m_prev - m_new)
                    a_ref[rs] = alpha
                ps = [jnp.exp2(c - m_new) for c in cols]
                psum = functools.reduce(lambda x, y: x + y, ps)
                if first:
                    l_ref[rs] = psum
                else:
                    l_ref[rs] = alpha * l_ref[rs] + psum
                m_ref[rs] = m_new
                for c, pc_ in enumerate(ps):
                    p_ref[rs, c * LANES:(c + 1) * LANES] = pc_.astype(BF16)
            o = jnp.dot(p_ref[gs, 0:W], vb, preferred_element_type=F32)
            if first:
                acc_ref[gs, :] = o
            else:
                a = a_ref[gs, :]
                acc_ref[gs, 0:LANES] = a * acc_ref[gs, 0:LANES] + o[:, 0:LANES]
                acc_ref[gs, LANES:] = a * acc_ref[gs, LANES:] + o[:, LANES:]
            if g + AHEAD < G:
                s_groups[g + AHEAD] = scores(g + AHEAD)

    colh = lax.broadcasted_iota(jnp.int32, (TQ, TK), 1)
    bias_h = jnp.where(lax.broadcasted_iota(jnp.int32, (TQ, LANES), 1) < N_META, 0.0, NEG).astype(F32)
    rq = lax.broadcasted_iota(jnp.int32, (TQ, TK), 0) // CHUNK
    bias_d = jnp.where(colh // CHUNK <= rq, 0.0, NEG).astype(F32)

    block([(0, LANES), (i + 1, TK)], [bias_h, bias_d], True)

    def body(t, carry):
        block([(2 * t + 1, TK), (2 * t + 2, TK)], None, False)
        return carry

    lax.fori_loop(0, i // 2, body, 0)

    @pl.when(i % 2 == 1)
    def _():
        block([(i, TK)], None, False)

    outs = []
    for h in range(H):
        rs = slice(h * TQ, (h + 1) * TQ)
        inv_l = 1.0 / jnp.sum(l_ref[rs], axis=-1, keepdims=True)
        o = (acc_ref[rs] * inv_l).astype(BF16)
        outs.append(jnp.dot(o, wuv_ref[h], preferred_element_type=F32))
    mla = jnp.concatenate(outs, axis=-1)
    o_ref[0] = _rms(mla, og_ref[...]).astype(o_ref.dtype)


def _mla(projv, proj_meta, pc, ps, pcm, psm, qg, kvg, wuq, wukt, sel, wuv, og, TQ, G):
    B, S, _ = projv.shape
    TK = TQ
    NU = 1 + S // TK
    R = MLA_HEADS * TQ
    nq = wuq.shape[1]
    HP = MLA_HEADS * HEAD_PAD
    est = (2 * S * Q_LORA * 2 + 2 * TQ * nq * 4 * 2 + 2 * S * KAUG * 2 + 4 * S * LANES * 4
           + 2 * (Q_LORA * nq + 2 * MLA_HEADS * 128 * KV_LORA + LANES * LANES) * 2 + 2 * TQ * MLA_WIDTH * 2
           + NU * TK * KAUG * 2 + NU * HP * TK * 2 + (S // TQ) * R * HEAD_PAD * 2 + R * 2 * TK * 2
           + 3 * R * LANES * 4 + R * KV_LORA * 4
           + 3 * R * TK * 4 + S * KAUG * 4 * 2)
    kern = functools.partial(_mla_kernel, TQ=TQ, TK=TK, G=G)
    const2 = lambda b, i: (0, 0)
    const3 = lambda b, i: (0, 0, 0)
    return pl.pallas_call(
        kern,
        grid=(B, S // TQ),
        in_specs=[
            pl.BlockSpec((1, S, Q_LORA), lambda b, i: (b, 0, COL_CQ // Q_LORA)),
            pl.BlockSpec((1, S, KAUG), lambda b, i: (b, 0, COL_CKV // KAUG)),
            pl.BlockSpec((N_META, KAUG), lambda b, i: (0, COL_CKV // KAUG)),
            pl.BlockSpec((S, LANES), const2),
            pl.BlockSpec((S, LANES), const2),
            pl.BlockSpec((N_META, LANES), const2),
            pl.BlockSpec((N_META, LANES), const2),
            pl.BlockSpec((1, Q_LORA), const2),
            pl.BlockSpec((1, KV_LORA), const2),
            pl.BlockSpec((Q_LORA, nq), const2),
            pl.BlockSpec((MLA_HEADS * MLA_NOPE_DIM, KV_LORA), const2),
            pl.BlockSpec((LANES, LANES), const2),
            pl.BlockSpec((MLA_HEADS, KV_LORA, MLA_V_DIM), const3),
            pl.BlockSpec((1, MLA_WIDTH), const2),
        ],
        out_specs=pl.BlockSpec((1, TQ, MLA_WIDTH), lambda b, i: (b, i, 0)),
        out_shape=jax.ShapeDtypeStruct((B, S, MLA_WIDTH), BF16),
        scratch_shapes=[
            pltpu.VMEM((NU * TK, KAUG), BF16),
            pltpu.VMEM((NU, HP, TK), BF16),
            pltpu.VMEM((S // TQ, R, HEAD_PAD), BF16),
            pltpu.VMEM((R, 2 * TK), BF16),
            pltpu.VMEM((R, LANES), F32),
            pltpu.VMEM((R, LANES), F32),
            pltpu.VMEM((R, LANES), F32),
            pltpu.VMEM((R, KV_LORA), F32),
        ],
        compiler_params=pltpu.CompilerParams(
            dimension_semantics=("arbitrary", "arbitrary"),
            vmem_limit_bytes=_vmem_limit(est + (8 << 20))),
        name="mla",
    )(projv, projv, proj_meta, pc, ps, pcm, psm, qg, kvg, wuq, wukt, sel, wuv, og)


def _oproj_kernel(x_ref, a1_ref, a2_ref, w1_ref, w2_ref, o_ref):
    acc = jnp.dot(a1_ref[...], w1_ref[...], preferred_element_type=F32)
    acc = acc + jnp.dot(a2_ref[...], w2_ref[...], preferred_element_type=F32)
    o_ref[...] = x_ref[...] + acc


def _oproj(x, a1, a2, w1, w2, tm):
    m, d = x.shape
    k = a1.shape[1]
    est = 2 * tm * d * 4 * 2 + 2 * 2 * tm * k * 2 + 2 * 2 * k * d * 2 + tm * d * 4
    return pl.pallas_call(
        _oproj_kernel,
        grid=(m // tm,),
        in_specs=[
            pl.BlockSpec((tm, d), lambda i: (i, 0)),
            pl.BlockSpec((tm, k), lambda i: (i, 0)),
            pl.BlockSpec((tm, k), lambda i: (i, 0)),
            pl.BlockSpec((k, d), lambda i: (0, 0)),
            pl.BlockSpec((k, d), lambda i: (0, 0)),
        ],
        out_specs=pl.BlockSpec((tm, d), lambda i: (i, 0)),
        out_shape=jax.ShapeDtypeStruct((m, d), F32),
        compiler_params=pltpu.CompilerParams(
            dimension_semantics=("arbitrary",),
            vmem_limit_bytes=_vmem_limit(est + (8 << 20))),
        name="oproj",
    )(x, a1, a2, w1, w2)


def _mlp_kernel(h_ref, g_ref, wup_ref, wdn_ref, fg_ref, o_ref, m_ref, *, nchunks, nsplit):
    f = pl.program_id(1)
    last = pl.num_programs(1) - 1
    rc = m_ref.shape[0] // nchunks

    def ffn(m):
        fs = wup_ref.shape[1] // nsplit
        ups = [jnp.dot(m, wup_ref[:, s * fs:(s + 1) * fs], preferred_element_type=F32)
               for s in range(nsplit)]
        out = None
        for s in range(nsplit):
            a = jnp.square(jnp.maximum(ups[s], 0.0)).astype(BF16)
            d = jnp.dot(a, wdn_ref[s * fs:(s + 1) * fs, :], preferred_element_type=F32)
            out = d if out is None else out + d
        return out

    @pl.when(f == 0)
    def _():
        m = _rms(h_ref[...], g_ref[...]).astype(BF16)
        m_ref[...] = m
        o_ref[...] = ffn(m)

    @pl.when((f > 0) & (f < last))
    def _():
        o_ref[...] += ffn(m_ref[...])

    @pl.when(f == last)
    def _():
        for c in range(nchunks):
            rows = slice(c * rc, (c + 1) * rc)
            y = o_ref[rows, :] + ffn(m_ref[rows, :]) + h_ref[rows, :]
            o_ref[rows, :] = _rms(y, fg_ref[...])


def _mlp(h, g, wup, wdn, fg, tm, tf, nchunks, nsplit):
    m, d = h.shape
    ff = wup.shape[1]
    est = 2 * tm * d * 4 * 2 + tm * d * 2 + 2 * 2 * d * tf * 2 + tm * tf * 6
    kern = functools.partial(_mlp_kernel, nchunks=nchunks, nsplit=nsplit)
    return pl.pallas_call(
        kern,
        grid=(m // tm, ff // tf),
        in_specs=[
            pl.BlockSpec((tm, d), lambda i, f: (i, 0)),
            pl.BlockSpec((1, d), lambda i, f: (0, 0)),
            pl.BlockSpec((d, tf), lambda i, f: (0, f)),
            pl.BlockSpec((tf, d), lambda i, f: (f, 0)),
            pl.BlockSpec((1, d), lambda i, f: (0, 0)),
        ],
        out_specs=pl.BlockSpec((tm, d), lambda i, f: (i, 0)),
        out_shape=jax.ShapeDtypeStruct((m, d), F32),
        scratch_shapes=[pltpu.VMEM((tm, d), BF16)],
        compiler_params=pltpu.CompilerParams(
            dimension_semantics=("arbitrary", "arbitrary"),
            vmem_limit_bytes=_vmem_limit(est + (8 << 20))),
        name="mlp",
    )(h, g, wup, wdn, fg)


def _rope_tables(pos, dim):
    inv = 1.0 / (ROPE_BASE ** (jnp.arange(0, dim, 2, dtype=F32) / dim))
    ang = pos[:, None] * inv[None, :]
    return jnp.cos(ang), jnp.sin(ang)


def kernel(x, meta_tokens, norm_mix_g, w_in, ret_out_g, q_norm_g, w_uq, kv_norm_g,
           w_uk, w_uv, mla_out_g, w_o, norm_mlp_g, w_up, w_down, final_norm_g):
    B, S, D = x.shape
    assert norm_mix_g.shape[0] == 1 and meta_tokens.shape == (N_META, D)
    M = B * S
    xm = x.reshape(M, D)

    w_in_aug = jnp.pad(w_in[0].astype(BF16), ((0, 0), (0, PROJ_WIDTH - w_in.shape[2])))
    wq3 = w_uq[0].reshape(Q_LORA, MLA_HEADS, MLA_QK_DIM)
    wuq = jnp.concatenate(
        [wq3, jnp.zeros((Q_LORA, MLA_HEADS, HEAD_PAD - MLA_QK_DIM), wq3.dtype)],
        axis=2).reshape(Q_LORA, MLA_HEADS * HEAD_PAD).astype(BF16)
    wukt = w_uk[0].T.astype(BF16)
    sel = jnp.zeros((LANES, LANES), F32).at[:MLA_ROPE_DIM, :MLA_ROPE_DIM].set(
        jnp.eye(MLA_ROPE_DIM, dtype=F32)).astype(BF16)
    wuv = jnp.transpose(w_uv[0].reshape(KV_LORA, MLA_HEADS, MLA_V_DIM), (1, 0, 2)).astype(BF16)
    wo1 = w_o[0, :RET_WIDTH].astype(BF16)
    wo2 = w_o[0, RET_WIDTH:].astype(BF16)
    wup = w_up[0].astype(BF16)
    wdn = w_down[0].astype(BF16)

    pos = jnp.arange(S, dtype=F32) + float(N_META)
    posm = jnp.arange(N_META, dtype=F32)

    def ret_tables(p):
        c, s = _rope_tables(p, RET_HEAD_DIM)
        return jnp.concatenate([c, c], 1), jnp.concatenate([-s, s], 1)

    def mla_tables(p):
        c, s = _rope_tables(p, MLA_ROPE_DIM)
        return jnp.concatenate([c, c, c, c], 1), jnp.concatenate([-s, s, -s, s], 1)

    cos2, sin2 = ret_tables(pos)
    cosm, sinm = ret_tables(posm)
    pc, ps = mla_tables(pos)
    pcm, psm = mla_tables(posm)

    g_mix = norm_mix_g[0].reshape(1, D)
    proj = _inproj(xm, g_mix, w_in_aug, cos2, sin2, tm=512)
    proj_meta = _inproj(meta_tokens, g_mix, w_in_aug, cosm, sinm, tm=N_META)
    projv = proj.reshape(B, S, PROJ_WIDTH)

    ret = _retention(projv, proj_meta, ret_out_g[0].reshape(1, RET_WIDTH), T=256)
    mla = _mla(projv, proj_meta, pc, ps, pcm, psm,
               q_norm_g[0].reshape(1, Q_LORA), kv_norm_g[0].reshape(1, KV_LORA),
               wuq, wukt, sel, wuv, mla_out_g[0].reshape(1, MLA_WIDTH), TQ=256, G=8)

    h = _oproj(xm, ret.reshape(M, RET_WIDTH), mla.reshape(M, MLA_WIDTH), wo1, wo2, tm=512)
    out = _mlp(h, norm_mlp_g[0].reshape(1, D), wup, wdn, final_norm_g.reshape(1, D),
               tm=1024, tf=1024, nchunks=2, nsplit=2)
    return out.reshape(B, S, D)
```

```python
import functools

import numpy as np
import jax
import jax.numpy as jnp
from jax import lax
from jax.experimental import pallas as pl
from jax.experimental.pallas import tpu as pltpu

F32 = jnp.float32
BF16 = jnp.bfloat16

CHUNK = 64
N_META = 16
RET_HEADS = 8
RET_HEAD_DIM = 128
MLA_HEADS = 8
MLA_V_DIM = 128
MLA_NOPE_DIM = 128
MLA_ROPE_DIM = 64
Q_LORA = 512
KV_LORA = 256
MLA_QK_DIM = MLA_NOPE_DIM + MLA_ROPE_DIM
ROPE_BASE = 10000.0
EPS = 1e-6
LOG2E = float(np.log2(np.e))

LANES = 128
RET_WIDTH = RET_HEADS * RET_HEAD_DIM
MLA_WIDTH = MLA_HEADS * MLA_V_DIM
COL_CQ = 4 * RET_WIDTH
COL_CKV = COL_CQ + Q_LORA
PROJ_WIDTH = 5120
KAUG = KV_LORA + LANES
HEAD_PAD = 256

VMEM_BYTES_V7X = 64 * 1024 * 1024
VMEM_LIMIT_CAP = VMEM_BYTES_V7X - 2 * 1024 * 1024


def _vmem_limit(nbytes):
    return int(min(VMEM_LIMIT_CAP, nbytes))


def _rms(x, g):
    ms = jnp.mean(x * x, axis=-1, keepdims=True)
    return x * lax.rsqrt(ms + EPS) * g


def _rope128(x, cos2, sin2):
    return x * cos2 + pltpu.roll(x, 64, 1) * sin2


def _inproj_kernel(x_ref, g_ref, w_ref, cos_ref, sin_ref, *rest, ncast):
    o_ref = rest[ncast]
    for src_ref, dst_ref in zip(rest[:ncast], rest[ncast + 1:]):
        dst_ref[...] = src_ref[...].astype(dst_ref.dtype)
    tn = RET_WIDTH
    u = _rms(x_ref[...], g_ref[...]).astype(BF16)
    cos2 = cos_ref[...]
    sin2 = sin_ref[...]

    def rope_heads(acc, scale):
        return jnp.concatenate(
            [_rope128(acc[:, h * LANES:(h + 1) * LANES], cos2, sin2) * scale
             for h in range(RET_HEADS)], axis=1)

    epilogues = {
        0: lambda acc: rope_heads(acc, RET_HEAD_DIM ** -0.5),
        1: lambda acc: rope_heads(acc, 1.0),
        3: lambda acc: acc * (1.0 / (1.0 + jnp.exp(-acc))),
    }
    for j in range(w_ref.shape[1] // tn):
        acc = jnp.dot(u, w_ref[:, j * tn:(j + 1) * tn], preferred_element_type=F32)
        if j in epilogues:
            acc = epilogues[j](acc)
        o_ref[:, j * tn:(j + 1) * tn] = acc.astype(o_ref.dtype)


def _inproj(x, g, w, cos2, sin2, tm, cast=()):
    m, d = x.shape
    n = w.shape[1]
    steps = m // tm
    assert n == 5 * RET_WIDTH and cos2.shape[0] % tm == 0
    assert all(c.shape[0] % (16 * steps) == 0 for c in cast)
    seq_tiles = cos2.shape[0] // tm
    slabs = [(c.shape[0] // steps, c.shape[1]) for c in cast]
    est = (2 * tm * d * 4 + tm * d * 2 + d * n * 2 + 2 * tm * n * 2 + 3 * tm * RET_WIDTH * 4
           + tm * d * 4 + 4 * tm * LANES * 4 + sum(2 * r * c * 6 for r, c in slabs))
    outs = pl.pallas_call(
        functools.partial(_inproj_kernel, ncast=len(cast)),
        grid=(steps,),
        in_specs=[
            pl.BlockSpec((tm, d), lambda i: (i, 0)),
            pl.BlockSpec((1, d), lambda i: (0, 0)),
            pl.BlockSpec((d, n), lambda i: (0, 0), pipeline_mode=pl.Buffered(1)),
            pl.BlockSpec((tm, LANES), lambda i: (i % seq_tiles, 0)),
            pl.BlockSpec((tm, LANES), lambda i: (i % seq_tiles, 0)),
        ] + [pl.BlockSpec(s, lambda i: (i, 0)) for s in slabs],
        out_specs=[pl.BlockSpec((tm, n), lambda i: (i, 0))]
        + [pl.BlockSpec(s, lambda i: (i, 0)) for s in slabs],
        out_shape=[jax.ShapeDtypeStruct((m, n), BF16)]
        + [jax.ShapeDtypeStruct(c.shape, BF16) for c in cast],
        compiler_params=pltpu.CompilerParams(
            dimension_semantics=("arbitrary",),
            vmem_limit_bytes=_vmem_limit(est + (8 << 20))),
        name="inproj",
    )(x, g, w, cos2, sin2, *cast)
    return outs


def _retention_kernel(q_ref, k_ref, v_ref, km_ref, vm_ref,
                      o_ref,
                      state_ref, d_ref, wq_ref, wk_ref, *, T):
    b = pl.program_id(0)
    t = pl.program_id(1)
    H = RET_HEADS
    dk = RET_HEAD_DIM

    def log_gamma(h):
        e = jnp.full((1, 1), -5.0 - h, F32)
        return jnp.log(1.0 - jnp.exp2(e))

    @pl.when((b == 0) & (t == 0))
    def _():
        ii = lax.broadcasted_iota(jnp.int32, (T, T), 0)
        jj = lax.broadcasted_iota(jnp.int32, (T, T), 1)
        dist = jnp.abs(ii - jj).astype(F32)
        vis = (jj // CHUNK) <= (ii // CHUNK)
        row = lax.broadcasted_iota(jnp.int32, (T, LANES), 0).astype(F32)
        for h in range(H):
            lg = log_gamma(h)
            d_ref[h] = jnp.where(vis, jnp.exp(lg * dist), 0.0)
            wq_ref[h] = jnp.exp(lg * (row + 1.0))
            wk_ref[h] = jnp.exp(lg * (T - 1.0 - row))

    @pl.when(t == 0)
    def _():
        rowm = lax.broadcasted_iota(jnp.int32, (N_META, LANES), 0).astype(F32)
        for h in range(H):
            lg = log_gamma(h)
            km = km_ref[:, h * dk:(h + 1) * dk].astype(F32) * jnp.exp(lg * (N_META - 1.0 - rowm))
            vm = vm_ref[:, h * dk:(h + 1) * dk]
            state_ref[h] = lax.dot_general(
                km.astype(BF16), vm, (((0,), (0,)), ((), ())),
                preferred_element_type=F32)

    pre = []
    for h in range(H):
        cs = slice(h * dk, (h + 1) * dk)
        lg = log_gamma(h)
        q = q_ref[0, :, cs]
        k = k_ref[0, :, cs]
        v = v_ref[0, :, cs]
        s = lax.dot_general(q, k, (((1,), (1,)), ((), ())), preferred_element_type=F32)
        st = state_ref[h]
        inter = jnp.dot((q.astype(F32) * wq_ref[h]).astype(BF16), st.astype(BF16),
                        preferred_element_type=F32)
        kv = lax.dot_general((k.astype(F32) * wk_ref[h]).astype(BF16), v,
                             (((0,), (0,)), ((), ())), preferred_element_type=F32)
        state_ref[h] = st * jnp.exp(lg * float(T)) + kv
        pre.append((s, inter))
    for h in range(H):
        cs = slice(h * dk, (h + 1) * dk)
        s, inter = pre[h]
        sd = (s * d_ref[h]).astype(BF16)
        o = jnp.dot(sd, v_ref[0, :, cs], preferred_element_type=F32) + inter
        mu = jnp.mean(o, axis=-1, keepdims=True)
        oc = o - mu
        var = jnp.mean(oc * oc, axis=-1, keepdims=True)
        o_ref[0, :, cs] = (oc * lax.rsqrt(var + EPS)).astype(o_ref.dtype)


def _retention(projv, proj_meta, T):
    B, S, _ = projv.shape
    W = RET_WIDTH
    est = (2 * 4 * T * W * 2 + 2 * T * W * 2
           + RET_HEADS * (T * T + 2 * T * LANES + 128 * 128) * 4 + 16 * T * T * 4)
    kern = functools.partial(_retention_kernel, T=T)
    return pl.pallas_call(
        kern,
        grid=(B, S // T),
        in_specs=[
            pl.BlockSpec((1, T, W), lambda b, t: (b, t, 0)),
            pl.BlockSpec((1, T, W), lambda b, t: (b, t, 1)),
            pl.BlockSpec((1, T, W), lambda b, t: (b, t, 2)),
            pl.BlockSpec((N_META, W), lambda b, t: (0, 1)),
            pl.BlockSpec((N_META, W), lambda b, t: (0, 2)),
        ],
        out_specs=pl.BlockSpec((1, T, W), lambda b, t: (b, t, 0)),
        out_shape=jax.ShapeDtypeStruct((B, S, W), BF16),
        scratch_shapes=[
            pltpu.VMEM((RET_HEADS, RET_HEAD_DIM, RET_HEAD_DIM), F32),
            pltpu.VMEM((RET_HEADS, T, T), F32),
            pltpu.VMEM((RET_HEADS, T, LANES), F32),
            pltpu.VMEM((RET_HEADS, T, LANES), F32),
        ],
        compiler_params=pltpu.CompilerParams(
            dimension_semantics=("arbitrary", "arbitrary"),
            vmem_limit_bytes=_vmem_limit(est + (8 << 20))),
        name="retention",
    )(projv, projv, projv, proj_meta, proj_meta)


def _swap32(x):
    lane = lax.broadcasted_iota(jnp.int32, x.shape, 1)
    return jnp.where((lane % 64) < 32, pltpu.roll(x, 96, 1), pltpu.roll(x, 32, 1))


def _rope64x2(x, pc, ps):
    return x * pc + _swap32(x) * ps


def _mla_kernel(cq_ref, kv_ref, kvm_ref, pc_ref, ps_ref, pcm_ref, psm_ref,
                qg_ref, kvg_ref, wuq_ref, wukt_ref, sel_ref, wuv_ref, og_ref,
                o_ref,
                kaug_ref, kt_ref, qh_ref, p_ref, m_ref, l_ref, a_ref, acc_ref,
                *, TQ, TK, G):
    i = pl.program_id(1)
    H = MLA_HEADS
    NU = kt_ref.shape[0]
    qscale = (MLA_QK_DIM ** -0.5) * LOG2E
    NEG = -1e30

    def prep_keys(kv, pc, ps):
        c = _rms(kv[:, :KV_LORA], kvg_ref[...])
        kr = _rope64x2(kv[:, KV_LORA:], pc, ps)
        return c.astype(BF16), kr.astype(BF16)

    @pl.when(i == 0)
    def _():
        kaug_ref[0:TK, :] = jnp.zeros((TK, KAUG), BF16)
        cm, krm = prep_keys(kvm_ref[...].astype(F32), pcm_ref[...], psm_ref[...])
        kaug_ref[0:N_META, 0:KV_LORA] = cm
        kaug_ref[0:N_META, KV_LORA:] = krm
        c, kr = prep_keys(kv_ref[0].astype(F32), pc_ref[...], ps_ref[...])
        kaug_ref[TK:, 0:KV_LORA] = c
        kaug_ref[TK:, KV_LORA:] = kr
        nt = (((1,), (1,)), ((), ()))
        for u in range(NU):
            ku = kaug_ref[u * TK:(u + 1) * TK, :]
            kn = lax.dot_general(wukt_ref[...], ku[:, 0:KV_LORA], nt,
                                 preferred_element_type=F32).astype(BF16)
            krt = lax.dot_general(sel_ref[...], ku[:, KV_LORA:], nt,
                                  preferred_element_type=F32).astype(BF16)
            for h in range(H):
                kt_ref[u, h * HEAD_PAD:h * HEAD_PAD + LANES, :] = kn[h * LANES:(h + 1) * LANES]
                kt_ref[u, h * HEAD_PAD + LANES:(h + 1) * HEAD_PAD, :] = krt

        QC = 2 * TQ
        for c in range(cq_ref.shape[1] // QC):
            rows = slice(c * QC, (c + 1) * QC)
            cq = _rms(cq_ref[0, rows, :].astype(F32), qg_ref[...]).astype(BF16)
            q = jnp.dot(cq, wuq_ref[...], preferred_element_type=F32)
            pcq = pc_ref[rows, :]
            psq = ps_ref[rows, :]
            for h in range(H):
                c0 = h * HEAD_PAD
                qn = (q[:, c0:c0 + LANES] * qscale).astype(BF16)
                rp = (_rope64x2(q[:, c0 + LANES:c0 + HEAD_PAD], pcq, psq) * qscale).astype(BF16)
                for sb in range(QC // TQ):
                    qb = c * (QC // TQ) + sb
                    qh_ref[qb, h * TQ:(h + 1) * TQ, 0:LANES] = qn[sb * TQ:(sb + 1) * TQ]
                    qh_ref[qb, h * TQ:(h + 1) * TQ, LANES:] = rp[sb * TQ:(sb + 1) * TQ]

    def block(units, bias, first):
        W = sum(w for _, w in units)
        def key_rows(u, w):
            if isinstance(u, int):
                return kaug_ref[u * TK:u * TK + w, 0:KV_LORA]
            return kaug_ref[pl.ds(pl.multiple_of(u * TK, TK), w), 0:KV_LORA]

        vb = jnp.concatenate([key_rows(u, w) for u, w in units], axis=0)
        HG = H // G
        AHEAD = 2

        def scores(g):
            out = []
            for h in range(g * HG, (g + 1) * HG):
                qh = qh_ref[i, h * TQ:(h + 1) * TQ, :]
                out.append([jnp.dot(qh, kt_ref[u, h * HEAD_PAD:(h + 1) * HEAD_PAD, 0:w],
                                    preferred_element_type=F32) for u, w in units])
            return out

        s_groups = {g: scores(g) for g in range(min(AHEAD, G))}
        for g in range(G):
            gs = slice(g * HG * TQ, (g + 1) * HG * TQ)
            s_g = s_groups.pop(g)
            for hh in range(HG):
                h = g * HG + hh
                rs = slice(h * TQ, (h + 1) * TQ)
                cols = []
                for ui, s in enumerate(s_g[hh]):
                    if bias is not None and bias[ui] is not None:
                        s = s + bias[ui]
                    cols += [s[:, c * LANES:(c + 1) * LANES] for c in range(s.shape[1] // LANES)]
                m_cur = jnp.max(functools.reduce(jnp.maximum, cols), axis=-1, keepdims=True)
                if first:
                    m_new = jnp.broadcast_to(m_cur, (TQ, LANES))
                else:
                    m_prev = m_ref[rs]
                    m_new = jnp.maximum(m_prev, m_cur)
                    alpha = jnp.exp2(m_prev - m_new)
                    a_ref[rs] = alpha
                ps = [jnp.exp2(c - m_new) for c in cols]
                psum = functools.reduce(lambda x, y: x + y, ps)
                if first:
                    l_ref[rs] = psum
                else:
                    l_ref[rs] = alpha * l_ref[rs] + psum
                m_ref[rs] = m_new
                for c, pc_ in enumerate(ps):
                    p_ref[rs, c * LANES:(c + 1) * LANES] = pc_.astype(BF16)
            o = jnp.dot(p_ref[gs, 0:W], vb, preferred_element_type=F32)
            if first:
                acc_ref[gs, :] = o
            else:
                a = a_ref[gs, :]
                acc_ref[gs, 0:LANES] = a * acc_ref[gs, 0:LANES] + o[:, 0:LANES]
                acc_ref[gs, LANES:] = a * acc_ref[gs, LANES:] + o[:, LANES:]
            if g + AHEAD < G:
                s_groups[g + AHEAD] = scores(g + AHEAD)

    colh = lax.broadcasted_iota(jnp.int32, (TQ, TK), 1)
    bias_h = jnp.where(lax.broadcasted_iota(jnp.int32, (TQ, LANES), 1) < N_META, 0.0, NEG).astype(F32)
    rq = lax.broadcasted_iota(jnp.int32, (TQ, TK), 0) // CHUNK
    bias_d = jnp.where(colh // CHUNK <= rq, 0.0, NEG).astype(F32)

    block([(0, LANES), (i + 1, TK)], [bias_h, bias_d], True)

    def body(t, carry):
        block([(2 * t + 1, TK), (2 * t + 2, TK)], None, False)
        return carry

    lax.fori_loop(0, i // 2, body, 0)

    @pl.when(i % 2 == 1)
    def _():
        block([(i, TK)], None, False)

    outs = []
    for h in range(H):
        rs = slice(h * TQ, (h + 1) * TQ)
        inv_l = 1.0 / jnp.sum(l_ref[rs], axis=-1, keepdims=True)
        o = (acc_ref[rs] * inv_l).astype(BF16)
        outs.append(jnp.dot(o, wuv_ref[h], preferred_element_type=F32))
    mla = jnp.concatenate(outs, axis=-1)
    o_ref[0] = _rms(mla, og_ref[...]).astype(o_ref.dtype)


def _mla(projv, proj_meta, pc, ps, pcm, psm, qg, kvg, wuq, wukt, sel, wuv, og, TQ, G):
    B, S, _ = projv.shape
    TK = TQ
    NU = 1 + S // TK
    R = MLA_HEADS * TQ
    nq = wuq.shape[1]
    HP = MLA_HEADS * HEAD_PAD
    est = (2 * S * Q_LORA * 2 + 2 * TQ * nq * 4 * 2 + 2 * S * KAUG * 2 + 4 * S * LANES * 4
           + 2 * (Q_LORA * nq + 2 * MLA_HEADS * 128 * KV_LORA + LANES * LANES) * 2 + 2 * TQ * MLA_WIDTH * 2
           + NU * TK * KAUG * 2 + NU * HP * TK * 2 + (S // TQ) * R * HEAD_PAD * 2 + R * 2 * TK * 2
           + 3 * R * LANES * 4 + R * KV_LORA * 4
           + 3 * R * TK * 4 + S * KAUG * 4 * 2)
    kern = functools.partial(_mla_kernel, TQ=TQ, TK=TK, G=G)
    const2 = lambda b, i: (0, 0)
    const3 = lambda b, i: (0, 0, 0)
    return pl.pallas_call(
        kern,
        grid=(B, S // TQ),
        in_specs=[
            pl.BlockSpec((1, S, Q_LORA), lambda b, i: (b, 0, COL_CQ // Q_LORA)),
            pl.BlockSpec((1, S, KAUG), lambda b, i: (b, 0, COL_CKV // KAUG)),
            pl.BlockSpec((N_META, KAUG), lambda b, i: (0, COL_CKV // KAUG)),
            pl.BlockSpec((S, LANES), const2),
            pl.BlockSpec((S, LANES), const2),
            pl.BlockSpec((N_META, LANES), const2),
            pl.BlockSpec((N_META, LANES), const2),
            pl.BlockSpec((1, Q_LORA), const2),
            pl.BlockSpec((1, KV_LORA), const2),
            pl.BlockSpec((Q_LORA, nq), const2),
            pl.BlockSpec((MLA_HEADS * MLA_NOPE_DIM, KV_LORA), const2),
            pl.BlockSpec((LANES, LANES), const2),
            pl.BlockSpec((MLA_HEADS, KV_LORA, MLA_V_DIM), const3),
            pl.BlockSpec((1, MLA_WIDTH), const2),
        ],
        out_specs=pl.BlockSpec((1, TQ, MLA_WIDTH), lambda b, i: (b, i, 0)),
        out_shape=jax.ShapeDtypeStruct((B, S, MLA_WIDTH), BF16),
        scratch_shapes=[
            pltpu.VMEM((NU * TK, KAUG), BF16),
            pltpu.VMEM((NU, HP, TK), BF16),
            pltpu.VMEM((S // TQ, R, HEAD_PAD), BF16),
            pltpu.VMEM((R, 2 * TK), BF16),
            pltpu.VMEM((R, LANES), F32),
            pltpu.VMEM((R, LANES), F32),
            pltpu.VMEM((R, LANES), F32),
            pltpu.VMEM((R, KV_LORA), F32),
        ],
        compiler_params=pltpu.CompilerParams(
            dimension_semantics=("arbitrary", "arbitrary"),
            vmem_limit_bytes=_vmem_limit(est + (8 << 20))),
        name="mla",
    )(projv, projv, proj_meta, pc, ps, pcm, psm, qg, kvg, wuq, wukt, sel, wuv, og)


def _oproj_kernel(x_ref, ret_ref, gate_ref, a2_ref, w1_ref, w2_ref, gout_ref, o_ref):
    acc = jnp.dot(a2_ref[...], w2_ref[...], preferred_element_type=F32)
    a1 = (gate_ref[...].astype(F32) * (ret_ref[...].astype(F32) * gout_ref[...])).astype(BF16)
    acc = acc + jnp.dot(a1, w1_ref[...], preferred_element_type=F32)
    o_ref[...] = x_ref[...] + acc


def _oproj(x, ret, proj, a2, w, gout, tm):
    m, d = x.shape
    k = ret.shape[1]
    est = (2 * tm * d * 4 * 2 + 3 * 2 * tm * k * 2 + 2 * k * d * 2 + tm * d * 4
           + 3 * tm * k * 4)
    resident = pl.Buffered(1)
    return pl.pallas_call(
        _oproj_kernel,
        grid=(m // tm,),
        in_specs=[
            pl.BlockSpec((tm, d), lambda i: (i, 0)),
            pl.BlockSpec((tm, k), lambda i: (i, 0)),
            pl.BlockSpec((tm, k), lambda i: (i, 3)),
            pl.BlockSpec((tm, k), lambda i: (i, 0)),
            pl.BlockSpec((k, d), lambda i: (0, 0), pipeline_mode=resident),
            pl.BlockSpec((k, d), lambda i: (1, 0), pipeline_mode=resident),
            pl.BlockSpec((1, k), lambda i: (0, 0)),
        ],
        out_specs=pl.BlockSpec((tm, d), lambda i: (i, 0)),
        out_shape=jax.ShapeDtypeStruct((m, d), F32),
        compiler_params=pltpu.CompilerParams(
            dimension_semantics=("arbitrary",),
            vmem_limit_bytes=_vmem_limit(est + (8 << 20))),
        name="oproj",
    )(x, ret, proj, a2, w, w, gout)


def _mlp_kernel(h_ref, g_ref, wup_ref, wdn_ref, fg_ref, o_ref, m_ref, *, nchunks, nsplit):
    f = pl.program_id(1)
    last = pl.num_programs(1) - 1
    rc = m_ref.shape[0] // nchunks

    def ffn(m):
        fs = wup_ref.shape[1] // nsplit
        ups = [jnp.dot(m, wup_ref[:, s * fs:(s + 1) * fs], preferred_element_type=F32)
               for s in range(nsplit)]
        out = None
        for s in range(nsplit):
            a = jnp.square(jnp.maximum(ups[s], 0.0)).astype(BF16)
            d = jnp.dot(a, wdn_ref[s * fs:(s + 1) * fs, :], preferred_element_type=F32)
            out = d if out is None else out + d
        return out

    @pl.when(f == 0)
    def _():
        m = _rms(h_ref[...], g_ref[...]).astype(BF16)
        m_ref[...] = m
        o_ref[...] = ffn(m)

    @pl.when((f > 0) & (f < last))
    def _():
        o_ref[...] += ffn(m_ref[...])

    @pl.when(f == last)
    def _():
        for c in range(nchunks):
            rows = slice(c * rc, (c + 1) * rc)
            y = o_ref[rows, :] + ffn(m_ref[rows, :]) + h_ref[rows, :]
            o_ref[rows, :] = _rms(y, fg_ref[...])


def _mlp(h, g, wup, wdn, fg, tm, tf, nchunks, nsplit):
    m, d = h.shape
    ff = wup.shape[1]
    est = 2 * tm * d * 4 * 2 + tm * d * 2 + 2 * 2 * d * tf * 2 + tm * tf * 6
    kern = functools.partial(_mlp_kernel, nchunks=nchunks, nsplit=nsplit)
    return pl.pallas_call(
        kern,
        grid=(m // tm, ff // tf),
        in_specs=[
            pl.BlockSpec((tm, d), lambda i, f: (i, 0)),
            pl.BlockSpec((1, d), lambda i, f: (0, 0)),
            pl.BlockSpec((d, tf), lambda i, f: (0, f)),
            pl.BlockSpec((tf, d), lambda i, f: (f, 0)),
            pl.BlockSpec((1, d), lambda i, f: (0, 0)),
        ],
        out_specs=pl.BlockSpec((tm, d), lambda i, f: (i, 0)),
        out_shape=jax.ShapeDtypeStruct((m, d), F32),
        scratch_shapes=[pltpu.VMEM((tm, d), BF16)],
        compiler_params=pltpu.CompilerParams(
            dimension_semantics=("arbitrary", "arbitrary"),
            vmem_limit_bytes=_vmem_limit(est + (8 << 20))),
        name="mlp",
    )(h, g, wup, wdn, fg)


def _rope_tables(pos, dim):
    inv = 1.0 / (ROPE_BASE ** (jnp.arange(0, dim, 2, dtype=F32) / dim))
    ang = pos[:, None] * inv[None, :]
    return jnp.cos(ang), jnp.sin(ang)


def kernel(x, meta_tokens, norm_mix_g, w_in, ret_out_g, q_norm_g, w_uq, kv_norm_g,
           w_uk, w_uv, mla_out_g, w_o, norm_mlp_g, w_up, w_down, final_norm_g):
    B, S, D = x.shape
    assert norm_mix_g.shape[0] == 1 and meta_tokens.shape == (N_META, D)
    M = B * S
    xm = x.reshape(M, D)

    w_in_aug = jnp.pad(w_in[0].astype(BF16), ((0, 0), (0, PROJ_WIDTH - w_in.shape[2])))
    wq3 = w_uq[0].reshape(Q_LORA, MLA_HEADS, MLA_QK_DIM)
    wuq = jnp.concatenate(
        [wq3, jnp.zeros((Q_LORA, MLA_HEADS, HEAD_PAD - MLA_QK_DIM), wq3.dtype)],
        axis=2).reshape(Q_LORA, MLA_HEADS * HEAD_PAD).astype(BF16)
    wukt = w_uk[0].T.astype(BF16)
    sel = jnp.zeros((LANES, LANES), F32).at[:MLA_ROPE_DIM, :MLA_ROPE_DIM].set(
        jnp.eye(MLA_ROPE_DIM, dtype=F32)).astype(BF16)
    wuv = jnp.transpose(w_uv[0].reshape(KV_LORA, MLA_HEADS, MLA_V_DIM), (1, 0, 2)).astype(BF16)

    pos = jnp.arange(S, dtype=F32) + float(N_META)
    posm = jnp.arange(N_META, dtype=F32)

    def ret_tables(p):
        c, s = _rope_tables(p, RET_HEAD_DIM)
        return jnp.concatenate([c, c], 1), jnp.concatenate([-s, s], 1)

    def mla_tables(p):
        c, s = _rope_tables(p, MLA_ROPE_DIM)
        return jnp.concatenate([c, c, c, c], 1), jnp.concatenate([-s, s, -s, s], 1)

    cos2, sin2 = ret_tables(pos)
    cosm, sinm = ret_tables(posm)
    pc, ps = mla_tables(pos)
    pcm, psm = mla_tables(posm)

    g_mix = norm_mix_g[0].reshape(1, D)
    proj, wup, wdn, wo = _inproj(xm, g_mix, w_in_aug, cos2, sin2, tm=512,
                                 cast=(w_up[0], w_down[0], w_o[0]))
    proj_meta, = _inproj(meta_tokens, g_mix, w_in_aug, cosm, sinm, tm=N_META)
    projv = proj.reshape(B, S, PROJ_WIDTH)

    ret = _retention(projv, proj_meta, T=256)
    mla = _mla(projv, proj_meta, pc, ps, pcm, psm,
               q_norm_g[0].reshape(1, Q_LORA), kv_norm_g[0].reshape(1, KV_LORA),
               wuq, wukt, sel, wuv, mla_out_g[0].reshape(1, MLA_WIDTH), TQ=256, G=8)

    h = _oproj(xm, ret.reshape(M, RET_WIDTH), proj, mla.reshape(M, MLA_WIDTH), wo,
               ret_out_g[0].reshape(1, RET_WIDTH), tm=1024)
    out = _mlp(h, norm_mlp_g[0].reshape(1, D), wup, wdn, final_norm_g.reshape(1, D),
               tm=1024, tf=1024, nchunks=2, nsplit=2)
    return out.reshape(B, S, D)
```

```python
import functools

import numpy as np
import jax
import jax.numpy as jnp
from jax import lax
from jax.experimental import pallas as pl
from jax.experimental.pallas import tpu as pltpu

F32 = jnp.float32
BF16 = jnp.bfloat16

CHUNK = 64
N_META = 16
RET_HEADS = 8
RET_HEAD_DIM = 128
MLA_HEADS = 8
MLA_V_DIM = 128
MLA_NOPE_DIM = 128
MLA_ROPE_DIM = 64
Q_LORA = 512
KV_LORA = 256
MLA_QK_DIM = MLA_NOPE_DIM + MLA_ROPE_DIM
ROPE_BASE = 10000.0
EPS = 1e-6
LOG2E = float(np.log2(np.e))

LANES = 128
RET_WIDTH = RET_HEADS * RET_HEAD_DIM
MLA_WIDTH = MLA_HEADS * MLA_V_DIM
COL_CQ = 4 * RET_WIDTH
COL_CKV = COL_CQ + Q_LORA
PROJ_WIDTH = 5120
KAUG = KV_LORA + LANES
HEAD_PAD = 256

VMEM_BYTES_V7X = 64 * 1024 * 1024
VMEM_LIMIT_CAP = VMEM_BYTES_V7X - 2 * 1024 * 1024


def _vmem_limit(nbytes):
    return int(min(VMEM_LIMIT_CAP, nbytes))


def _rms(x, g):
    ms = jnp.mean(x * x, axis=-1, keepdims=True)
    return x * lax.rsqrt(ms + EPS) * g


def _rope128(x, cos2, sin2):
    return x * cos2 + pltpu.roll(x, 64, 1) * sin2


def _inproj_kernel(x_ref, g_ref, w_ref, cos_ref, sin_ref, *rest, ncast):
    o_ref = rest[ncast]
    for src_ref, dst_ref in zip(rest[:ncast], rest[ncast + 1:]):
        dst_ref[...] = src_ref[...].astype(dst_ref.dtype)
    tn = RET_WIDTH
    u = _rms(x_ref[...], g_ref[...]).astype(BF16)
    cos2 = cos_ref[...]
    sin2 = sin_ref[...]

    def rope_heads(acc, scale):
        return jnp.concatenate(
            [_rope128(acc[:, h * LANES:(h + 1) * LANES], cos2, sin2) * scale
             for h in range(RET_HEADS)], axis=1)

    epilogues = {
        0: lambda acc: rope_heads(acc, RET_HEAD_DIM ** -0.5),
        1: lambda acc: rope_heads(acc, 1.0),
        3: lambda acc: acc * (1.0 / (1.0 + jnp.exp(-acc))),
    }
    for j in range(w_ref.shape[1] // tn):
        acc = jnp.dot(u, w_ref[:, j * tn:(j + 1) * tn], preferred_element_type=F32)
        if j in epilogues:
            acc = epilogues[j](acc)
        o_ref[:, j * tn:(j + 1) * tn] = acc.astype(o_ref.dtype)


def _inproj(x, g, w, cos2, sin2, tm, cast=()):
    m, d = x.shape
    n = w.shape[1]
    steps = m // tm
    assert n == 5 * RET_WIDTH and cos2.shape[0] % tm == 0
    assert all(c.shape[0] % (16 * steps) == 0 for c in cast)
    seq_tiles = cos2.shape[0] // tm
    slabs = [(c.shape[0] // steps, c.shape[1]) for c in cast]
    est = (2 * tm * d * 4 + tm * d * 2 + d * n * 2 + 2 * tm * n * 2 + 3 * tm * RET_WIDTH * 4
           + tm * d * 4 + 4 * tm * LANES * 4 + sum(2 * r * c * 6 for r, c in slabs))
    outs = pl.pallas_call(
        functools.partial(_inproj_kernel, ncast=len(cast)),
        grid=(steps,),
        in_specs=[
            pl.BlockSpec((tm, d), lambda i: (i, 0)),
            pl.BlockSpec((1, d), lambda i: (0, 0)),
            pl.BlockSpec((d, n), lambda i: (0, 0), pipeline_mode=pl.Buffered(1)),
            pl.BlockSpec((tm, LANES), lambda i: (i % seq_tiles, 0)),
            pl.BlockSpec((tm, LANES), lambda i: (i % seq_tiles, 0)),
        ] + [pl.BlockSpec(s, lambda i: (i, 0)) for s in slabs],
        out_specs=[pl.BlockSpec((tm, n), lambda i: (i, 0))]
        + [pl.BlockSpec(s, lambda i: (i, 0)) for s in slabs],
        out_shape=[jax.ShapeDtypeStruct((m, n), BF16)]
        + [jax.ShapeDtypeStruct(c.shape, BF16) for c in cast],
        compiler_params=pltpu.CompilerParams(
            dimension_semantics=("arbitrary",),
            vmem_limit_bytes=_vmem_limit(est + (8 << 20))),
        name="inproj",
    )(x, g, w, cos2, sin2, *cast)
    return outs


def _retention_kernel(q_ref, k_ref, v_ref, km_ref, vm_ref,
                      o_ref,
                      state_ref, d_ref, wq_ref, wk_ref, *, T):
    b = pl.program_id(0)
    t = pl.program_id(1)
    H = RET_HEADS
    dk = RET_HEAD_DIM

    def log_gamma(h):
        e = jnp.full((1, 1), -5.0 - h, F32)
        return jnp.log(1.0 - jnp.exp2(e))

    @pl.when((b == 0) & (t == 0))
    def _():
        ii = lax.broadcasted_iota(jnp.int32, (T, T), 0)
        jj = lax.broadcasted_iota(jnp.int32, (T, T), 1)
        dist = jnp.abs(ii - jj).astype(F32)
        vis = (jj // CHUNK) <= (ii // CHUNK)
        row = lax.broadcasted_iota(jnp.int32, (T, LANES), 0).astype(F32)
        for h in range(H):
            lg = log_gamma(h)
            d_ref[h] = jnp.where(vis, jnp.exp(lg * dist), 0.0)
            wq_ref[h] = jnp.exp(lg * (row + 1.0))
            wk_ref[h] = jnp.exp(lg * (T - 1.0 - row))

    @pl.when(t == 0)
    def _():
        rowm = lax.broadcasted_iota(jnp.int32, (N_META, LANES), 0).astype(F32)
        for h in range(H):
            lg = log_gamma(h)
            km = km_ref[:, h * dk:(h + 1) * dk].astype(F32) * jnp.exp(lg * (N_META - 1.0 - rowm))
            vm = vm_ref[:, h * dk:(h + 1) * dk]
            state_ref[h] = lax.dot_general(
                km.astype(BF16), vm, (((0,), (0,)), ((), ())),
                preferred_element_type=F32)

    pre = []
    for h in range(H):
        cs = slice(h * dk, (h + 1) * dk)
        lg = log_gamma(h)
        q = q_ref[0, :, cs]
        k = k_ref[0, :, cs]
        v = v_ref[0, :, cs]
        s = lax.dot_general(q, k, (((1,), (1,)), ((), ())), preferred_element_type=F32)
        st = state_ref[h]
        inter = jnp.dot((q.astype(F32) * wq_ref[h]).astype(BF16), st.astype(BF16),
                        preferred_element_type=F32)
        kv = lax.dot_general((k.astype(F32) * wk_ref[h]).astype(BF16), v,
                             (((0,), (0,)), ((), ())), preferred_element_type=F32)
        state_ref[h] = st * jnp.exp(lg * float(T)) + kv
        pre.append((s, inter))
    for h in range(H):
        cs = slice(h * dk, (h + 1) * dk)
        s, inter = pre[h]
        sd = (s * d_ref[h]).astype(BF16)
        o = jnp.dot(sd, v_ref[0, :, cs], preferred_element_type=F32) + inter
        mu = jnp.mean(o, axis=-1, keepdims=True)
        oc = o - mu
        var = jnp.mean(oc * oc, axis=-1, keepdims=True)
        o_ref[0, :, cs] = (oc * lax.rsqrt(var + EPS)).astype(o_ref.dtype)


def _retention(projv, proj_meta, T):
    B, S, _ = projv.shape
    W = RET_WIDTH
    est = (2 * 4 * T * W * 2 + 2 * T * W * 2
           + RET_HEADS * (T * T + 2 * T * LANES + 128 * 128) * 4 + 16 * T * T * 4)
    kern = functools.partial(_retention_kernel, T=T)
    return pl.pallas_call(
        kern,
        grid=(B, S // T),
        in_specs=[
            pl.BlockSpec((1, T, W), lambda b, t: (b, t, 0)),
            pl.BlockSpec((1, T, W), lambda b, t: (b, t, 1)),
            pl.BlockSpec((1, T, W), lambda b, t: (b, t, 2)),
            pl.BlockSpec((N_META, W), lambda b, t: (0, 1)),
            pl.BlockSpec((N_META, W), lambda b, t: (0, 2)),
        ],
        out_specs=pl.BlockSpec((1, T, W), lambda b, t: (b, t, 0)),
        out_shape=jax.ShapeDtypeStruct((B, S, W), BF16),
        scratch_shapes=[
            pltpu.VMEM((RET_HEADS, RET_HEAD_DIM, RET_HEAD_DIM), F32),
            pltpu.VMEM((RET_HEADS, T, T), F32),
            pltpu.VMEM((RET_HEADS, T, LANES), F32),
            pltpu.VMEM((RET_HEADS, T, LANES), F32),
        ],
        compiler_params=pltpu.CompilerParams(
            dimension_semantics=("arbitrary", "arbitrary"),
            vmem_limit_bytes=_vmem_limit(est + (8 << 20))),
        name="retention",
    )(projv, projv, projv, proj_meta, proj_meta)


def _first_half32(shape):
    return (lax.broadcasted_iota(jnp.int32, shape, 1) % 64) < 32


def _rope64x2(x, pc, ps, first_half=None):
    if first_half is None:
        first_half = _first_half32(x.shape)
    swapped = jnp.where(first_half, pltpu.roll(x, 96, 1), pltpu.roll(x, 32, 1))
    return x * pc + swapped * ps


def _mla_kernel(cq_ref, kv_ref, kvm_ref, pc_ref, ps_ref, pcm_ref, psm_ref,
                qg_ref, kvg_ref, wuq_ref, wukt_ref, sel_ref, wuv_ref, og_ref,
                o_ref,
                kaug_ref, kt_ref, qh_ref, p_ref, m_ref, l_ref, a_ref, acc_ref,
                *, TQ, TK, G):
    i = pl.program_id(1)
    H = MLA_HEADS
    NU = kt_ref.shape[0]
    qscale = (MLA_QK_DIM ** -0.5) * LOG2E
    NEG = -1e30

    def prep_keys(kv, pc, ps):
        c = _rms(kv[:, :KV_LORA], kvg_ref[...])
        kr = _rope64x2(kv[:, KV_LORA:], pc, ps)
        return c.astype(BF16), kr.astype(BF16)

    @pl.when(i == 0)
    def _():
        kaug_ref[0:TK, :] = jnp.zeros((TK, KAUG), BF16)
        cm, krm = prep_keys(kvm_ref[...].astype(F32), pcm_ref[...], psm_ref[...])
        kaug_ref[0:N_META, 0:KV_LORA] = cm
        kaug_ref[0:N_META, KV_LORA:] = krm
        c, kr = prep_keys(kv_ref[0].astype(F32), pc_ref[...], ps_ref[...])
        kaug_ref[TK:, 0:KV_LORA] = c
        kaug_ref[TK:, KV_LORA:] = kr
        nt = (((1,), (1,)), ((), ()))
        for u in range(NU):
            ku = kaug_ref[u * TK:(u + 1) * TK, :]
            kn = lax.dot_general(wukt_ref[...], ku[:, 0:KV_LORA], nt,
                                 preferred_element_type=F32).astype(BF16)
            krt = lax.dot_general(sel_ref[...], ku[:, KV_LORA:], nt,
                                  preferred_element_type=F32).astype(BF16)
            for h in range(H):
                kt_ref[u, h * HEAD_PAD:h * HEAD_PAD + LANES, :] = kn[h * LANES:(h + 1) * LANES]
                kt_ref[u, h * HEAD_PAD + LANES:(h + 1) * HEAD_PAD, :] = krt

        QC = 2 * TQ
        first_half = _first_half32((QC, LANES))
        for c in range(cq_ref.shape[1] // QC):
            rows = slice(c * QC, (c + 1) * QC)
            cq = _rms(cq_ref[0, rows, :].astype(F32), qg_ref[...]).astype(BF16)
            q = jnp.dot(cq, wuq_ref[...], preferred_element_type=F32)
            pcq = pc_ref[rows, :]
            psq = ps_ref[rows, :]
            for h in range(H):
                c0 = h * HEAD_PAD
                qn = (q[:, c0:c0 + LANES] * qscale).astype(BF16)
                rp = (_rope64x2(q[:, c0 + LANES:c0 + HEAD_PAD], pcq, psq, first_half)
                      * qscale).astype(BF16)
                for sb in range(QC // TQ):
                    qb = c * (QC // TQ) + sb
                    qh_ref[qb, h * TQ:(h + 1) * TQ, 0:LANES] = qn[sb * TQ:(sb + 1) * TQ]
                    qh_ref[qb, h * TQ:(h + 1) * TQ, LANES:] = rp[sb * TQ:(sb + 1) * TQ]

    colh = lax.broadcasted_iota(jnp.int32, (TQ, TK), 1)
    bias_h = jnp.where(lax.broadcasted_iota(jnp.int32, (TQ, LANES), 1) < N_META, 0.0, NEG).astype(F32)
    rq = lax.broadcasted_iota(jnp.int32, (TQ, TK), 0) // CHUNK
    bias_d = jnp.where(colh // CHUNK <= rq, 0.0, NEG).astype(F32)
    HG = H // G
    AHEAD = 2

    def query_block(k):
        blocks = [([(0, LANES), (k + 1, TK)], [bias_h, bias_d], True)]
        blocks += [([(2 * t + 1, TK), (2 * t + 2, TK)], None, False) for t in range(k // 2)]
        if k % 2 == 1:
            blocks.append(([(k, TK)], None, False))
        items = [(units, bias, first, g) for units, bias, first in blocks for g in range(G)]

        def scores(item):
            units, _, _, g = item
            out = []
            for h in range(g * HG, (g + 1) * HG):
                qh = qh_ref[k, h * TQ:(h + 1) * TQ, :]
                out.append([jnp.dot(qh, kt_ref[u, h * HEAD_PAD:(h + 1) * HEAD_PAD, 0:w],
                                    preferred_element_type=F32) for u, w in units])
            return out

        def process(item, s_g):
            units, bias, first, g = item
            W = sum(w for _, w in units)
            vb = jnp.concatenate(
                [kaug_ref[u * TK:u * TK + w, 0:KV_LORA] for u, w in units], axis=0)
            gs = slice(g * HG * TQ, (g + 1) * HG * TQ)
            for hh in range(HG):
                h = g * HG + hh
                rs = slice(h * TQ, (h + 1) * TQ)
                cols = []
                for ui, s in enumerate(s_g[hh]):
                    if bias is not None and bias[ui] is not None:
                        s = s + bias[ui]
                    cols += [s[:, c * LANES:(c + 1) * LANES] for c in range(s.shape[1] // LANES)]
                m_cur = jnp.max(functools.reduce(jnp.maximum, cols), axis=-1, keepdims=True)
                if first:
                    m_new = jnp.broadcast_to(m_cur, (TQ, LANES))
                else:
                    m_prev = m_ref[rs]
                    m_new = jnp.maximum(m_prev, m_cur)
                    alpha = jnp.exp2(m_prev - m_new)
                    a_ref[rs] = alpha
                ps = [jnp.exp2(c - m_new) for c in cols]
                psum = functools.reduce(lambda x, y: x + y, ps)
                if first:
                    l_ref[rs] = psum
                else:
                    l_ref[rs] = alpha * l_ref[rs] + psum
                m_ref[rs] = m_new
                for c, pc_ in enumerate(ps):
                    p_ref[rs, c * LANES:(c + 1) * LANES] = pc_.astype(BF16)
            o = jnp.dot(p_ref[gs, 0:W], vb, preferred_element_type=F32)
            if first:
                acc_ref[gs, :] = o
            else:
                a = a_ref[gs, :]
                acc_ref[gs, 0:LANES] = a * acc_ref[gs, 0:LANES] + o[:, 0:LANES]
                acc_ref[gs, LANES:] = a * acc_ref[gs, LANES:] + o[:, LANES:]

        pending = {j: scores(items[j]) for j in range(AHEAD)}
        for j, item in enumerate(items):
            process(item, pending.pop(j))
            if j + AHEAD < len(items):
                pending[j + AHEAD] = scores(items[j + AHEAD])

        outs = []
        for h in range(H):
            rs = slice(h * TQ, (h + 1) * TQ)
            inv_l = 1.0 / jnp.sum(l_ref[rs], axis=-1, keepdims=True)
            o = (acc_ref[rs] * inv_l).astype(BF16)
            outs.append(jnp.dot(o, wuv_ref[h], preferred_element_type=F32))
        mla = jnp.concatenate(outs, axis=-1)
        o_ref[0] = _rms(mla, og_ref[...]).astype(o_ref.dtype)

    for k in range(qh_ref.shape[0]):
        pl.when(i == k)(functools.partial(query_block, k))


def _mla(projv, proj_meta, pc, ps, pcm, psm, qg, kvg, wuq, wukt, sel, wuv, og, TQ, G):
    B, S, _ = projv.shape
    TK = TQ
    NU = 1 + S // TK
    R = MLA_HEADS * TQ
    nq = wuq.shape[1]
    HP = MLA_HEADS * HEAD_PAD
    est = (2 * S * Q_LORA * 2 + 2 * TQ * nq * 4 * 2 + 2 * S * KAUG * 2 + 4 * S * LANES * 4
           + 2 * (Q_LORA * nq + 2 * MLA_HEADS * 128 * KV_LORA + LANES * LANES) * 2 + 2 * TQ * MLA_WIDTH * 2
           + NU * TK * KAUG * 2 + NU * HP * TK * 2 + (S // TQ) * R * HEAD_PAD * 2 + R * 2 * TK * 2
           + 3 * R * LANES * 4 + R * KV_LORA * 4
           + 3 * R * TK * 4 + S * KAUG * 4 * 2)
    kern = functools.partial(_mla_kernel, TQ=TQ, TK=TK, G=G)
    const2 = lambda b, i: (0, 0)
    const3 = lambda b, i: (0, 0, 0)
    return pl.pallas_call(
        kern,
        grid=(B, S // TQ),
        in_specs=[
            pl.BlockSpec((1, S, Q_LORA), lambda b, i: (b, 0, COL_CQ // Q_LORA)),
            pl.BlockSpec((1, S, KAUG), lambda b, i: (b, 0, COL_CKV // KAUG)),
            pl.BlockSpec((N_META, KAUG), lambda b, i: (0, COL_CKV // KAUG)),
            pl.BlockSpec((S, LANES), const2),
            pl.BlockSpec((S, LANES), const2),
            pl.BlockSpec((N_META, LANES), const2),
            pl.BlockSpec((N_META, LANES), const2),
            pl.BlockSpec((1, Q_LORA), const2),
            pl.BlockSpec((1, KV_LORA), const2),
            pl.BlockSpec((Q_LORA, nq), const2),
            pl.BlockSpec((MLA_HEADS * MLA_NOPE_DIM, KV_LORA), const2),
            pl.BlockSpec((LANES, LANES), const2),
            pl.BlockSpec((MLA_HEADS, KV_LORA, MLA_V_DIM), const3),
            pl.BlockSpec((1, MLA_WIDTH), const2),
        ],
        out_specs=pl.BlockSpec((1, TQ, MLA_WIDTH), lambda b, i: (b, i, 0)),
        out_shape=jax.ShapeDtypeStruct((B, S, MLA_WIDTH), BF16),
        scratch_shapes=[
            pltpu.VMEM((NU * TK, KAUG), BF16),
            pltpu.VMEM((NU, HP, TK), BF16),
            pltpu.VMEM((S // TQ, R, HEAD_PAD), BF16),
            pltpu.VMEM((R, 2 * TK), BF16),
            pltpu.VMEM((R, LANES), F32),
            pltpu.VMEM((R, LANES), F32),
            pltpu.VMEM((R, LANES), F32),
            pltpu.VMEM((R, KV_LORA), F32),
        ],
        compiler_params=pltpu.CompilerParams(
            dimension_semantics=("arbitrary", "arbitrary"),
            vmem_limit_bytes=_vmem_limit(est + (8 << 20))),
        name="mla",
    )(projv, projv, proj_meta, pc, ps, pcm, psm, qg, kvg, wuq, wukt, sel, wuv, og)


def _oproj_kernel(x_ref, ret_ref, gate_ref, a2_ref, w1_ref, w2_ref, gout_ref, o_ref):
    acc = jnp.dot(a2_ref[...], w2_ref[...], preferred_element_type=F32)
    a1 = (gate_ref[...].astype(F32) * (ret_ref[...].astype(F32) * gout_ref[...])).astype(BF16)
    acc = acc + jnp.dot(a1, w1_ref[...], preferred_element_type=F32)
    o_ref[...] = x_ref[...] + acc


def _oproj(x, ret, proj, a2, w, gout, tm):
    m, d = x.shape
    k = ret.shape[1]
    est = (2 * tm * d * 4 * 2 + 3 * 2 * tm * k * 2 + 2 * k * d * 2 + tm * d * 4
           + 3 * tm * k * 4)
    resident = pl.Buffered(1)
    return pl.pallas_call(
        _oproj_kernel,
        grid=(m // tm,),
        in_specs=[
            pl.BlockSpec((tm, d), lambda i: (i, 0)),
            pl.BlockSpec((tm, k), lambda i: (i, 0)),
            pl.BlockSpec((tm, k), lambda i: (i, 3)),
            pl.BlockSpec((tm, k), lambda i: (i, 0)),
            pl.BlockSpec((k, d), lambda i: (0, 0), pipeline_mode=resident),
            pl.BlockSpec((k, d), lambda i: (1, 0), pipeline_mode=resident),
            pl.BlockSpec((1, k), lambda i: (0, 0)),
        ],
        out_specs=pl.BlockSpec((tm, d), lambda i: (i, 0)),
        out_shape=jax.ShapeDtypeStruct((m, d), F32),
        compiler_params=pltpu.CompilerParams(
            dimension_semantics=("arbitrary",),
            vmem_limit_bytes=_vmem_limit(est + (8 << 20))),
        name="oproj",
    )(x, ret, proj, a2, w, w, gout)


def _mlp_kernel(h_ref, g_ref, wup_ref, wdn_ref, fg_ref, o_ref, m_ref, *, nchunks, nsplit):
    f = pl.program_id(1)
    last = pl.num_programs(1) - 1
    rc = m_ref.shape[0] // nchunks

    def ffn(m):
        fs = wup_ref.shape[1] // nsplit
        ups = [jnp.dot(m, wup_ref[:, s * fs:(s + 1) * fs], preferred_element_type=F32)
               for s in range(nsplit)]
        out = None
        for s in range(nsplit):
            a = jnp.square(jnp.maximum(ups[s], 0.0)).astype(BF16)
            d = jnp.dot(a, wdn_ref[s * fs:(s + 1) * fs, :], preferred_element_type=F32)
            out = d if out is None else out + d
        return out

    @pl.when(f == 0)
    def _():
        for c in range(nchunks):
            rows = slice(c * rc, (c + 1) * rc)
            m = _rms(h_ref[rows, :], g_ref[...]).astype(BF16)
            m_ref[rows, :] = m
            o_ref[rows, :] = ffn(m)

    @pl.when((f > 0) & (f < last))
    def _():
        o_ref[...] += ffn(m_ref[...])

    @pl.when(f == last)
    def _():
        for c in range(nchunks):
            rows = slice(c * rc, (c + 1) * rc)
            y = o_ref[rows, :] + ffn(m_ref[rows, :]) + h_ref[rows, :]
            o_ref[rows, :] = _rms(y, fg_ref[...])


def _mlp(h, g, wup, wdn, fg, tm, tf, nchunks, nsplit):
    m, d = h.shape
    ff = wup.shape[1]
    est = 2 * tm * d * 4 * 2 + tm * d * 2 + 2 * 2 * d * tf * 2 + tm * tf * 6
    kern = functools.partial(_mlp_kernel, nchunks=nchunks, nsplit=nsplit)
    return pl.pallas_call(
        kern,
        grid=(m // tm, ff // tf),
        in_specs=[
            pl.BlockSpec((tm, d), lambda i, f: (i, 0)),
            pl.BlockSpec((1, d), lambda i, f: (0, 0)),
            pl.BlockSpec((d, tf), lambda i, f: (0, f)),
            pl.BlockSpec((tf, d), lambda i, f: (f, 0)),
            pl.BlockSpec((1, d), lambda i, f: (0, 0)),
        ],
        out_specs=pl.BlockSpec((tm, d), lambda i, f: (i, 0)),
        out_shape=jax.ShapeDtypeStruct((m, d), F32),
        scratch_shapes=[pltpu.VMEM((tm, d), BF16)],
        compiler_params=pltpu.CompilerParams(
            dimension_semantics=("arbitrary", "arbitrary"),
            vmem_limit_bytes=_vmem_limit(est + (8 << 20))),
        name="mlp",
    )(h, g, wup, wdn, fg)


def _rope_tables(pos, dim):
    inv = 1.0 / (ROPE_BASE ** (jnp.arange(0, dim, 2, dtype=F32) / dim))
    ang = pos[:, None] * inv[None, :]
    return jnp.cos(ang), jnp.sin(ang)


def kernel(x, meta_tokens, norm_mix_g, w_in, ret_out_g, q_norm_g, w_uq, kv_norm_g,
           w_uk, w_uv, mla_out_g, w_o, norm_mlp_g, w_up, w_down, final_norm_g):
    B, S, D = x.shape
    assert norm_mix_g.shape[0] == 1 and meta_tokens.shape == (N_META, D)
    M = B * S
    xm = x.reshape(M, D)

    w_in_aug = jnp.pad(w_in[0].astype(BF16), ((0, 0), (0, PROJ_WIDTH - w_in.shape[2])))
    wq3 = w_uq[0].reshape(Q_LORA, MLA_HEADS, MLA_QK_DIM)
    wuq = jnp.concatenate(
        [wq3, jnp.zeros((Q_LORA, MLA_HEADS, HEAD_PAD - MLA_QK_DIM), wq3.dtype)],
        axis=2).reshape(Q_LORA, MLA_HEADS * HEAD_PAD).astype(BF16)
    wukt = w_uk[0].T.astype(BF16)
    sel = jnp.zeros((LANES, LANES), F32).at[:MLA_ROPE_DIM, :MLA_ROPE_DIM].set(
        jnp.eye(MLA_ROPE_DIM, dtype=F32)).astype(BF16)
    wuv = jnp.transpose(w_uv[0].reshape(KV_LORA, MLA_HEADS, MLA_V_DIM), (1, 0, 2)).astype(BF16)

    pos = jnp.arange(S, dtype=F32) + float(N_META)
    posm = jnp.arange(N_META, dtype=F32)

    def ret_tables(p):
        c, s = _rope_tables(p, RET_HEAD_DIM)
        return jnp.concatenate([c, c], 1), jnp.concatenate([-s, s], 1)

    def mla_tables(p):
        c, s = _rope_tables(p, MLA_ROPE_DIM)
        return jnp.concatenate([c, c, c, c], 1), jnp.concatenate([-s, s, -s, s], 1)

    cos2, sin2 = ret_tables(pos)
    cosm, sinm = ret_tables(posm)
    pc, ps = mla_tables(pos)
    pcm, psm = mla_tables(posm)

    g_mix = norm_mix_g[0].reshape(1, D)
    proj, wup, wdn, wo = _inproj(xm, g_mix, w_in_aug, cos2, sin2, tm=512,
                                 cast=(w_up[0], w_down[0], w_o[0]))
    proj_meta, = _inproj(meta_tokens, g_mix, w_in_aug, cosm, sinm, tm=N_META)
    projv = proj.reshape(B, S, PROJ_WIDTH)

    ret = _retention(projv, proj_meta, T=256)
    mla = _mla(projv, proj_meta, pc, ps, pcm, psm,
               q_norm_g[0].reshape(1, Q_LORA), kv_norm_g[0].reshape(1, KV_LORA),
               wuq, wukt, sel, wuv, mla_out_g[0].reshape(1, MLA_WIDTH), TQ=256, G=8)

    h = _oproj(xm, ret.reshape(M, RET_WIDTH), proj, mla.reshape(M, MLA_WIDTH), wo,
               ret_out_g[0].reshape(1, RET_WIDTH), tm=1024)
    out = _mlp(h, norm_mlp_g[0].reshape(1, D), wup, wdn, final_norm_g.reshape(1, D),
               tm=1024, tf=1024, nchunks=2, nsplit=2)
    return out.reshape(B, S, D)
```

```python
import functools

import numpy as np
import jax
import jax.numpy as jnp
from jax import lax
from jax.experimental import pallas as pl
from jax.experimental.pallas import tpu as pltpu

F32 = jnp.float32
BF16 = jnp.bfloat16

CHUNK = 64
N_META = 16
RET_HEADS = 8
RET_HEAD_DIM = 128
MLA_HEADS = 8
MLA_V_DIM = 128
MLA_NOPE_DIM = 128
MLA_ROPE_DIM = 64
Q_LORA = 512
KV_LORA = 256
MLA_QK_DIM = MLA_NOPE_DIM + MLA_ROPE_DIM
ROPE_BASE = 10000.0
EPS = 1e-6
LOG2E = float(np.log2(np.e))

LANES = 128
RET_WIDTH = RET_HEADS * RET_HEAD_DIM
MLA_WIDTH = MLA_HEADS * MLA_V_DIM
COL_CQ = 4 * RET_WIDTH
COL_CKV = COL_CQ + Q_LORA
PROJ_WIDTH = 5120
KAUG = KV_LORA + LANES
HEAD_PAD = 256

VMEM_BYTES_V7X = 64 * 1024 * 1024
VMEM_LIMIT_CAP = VMEM_BYTES_V7X - 2 * 1024 * 1024


def _vmem_limit(nbytes):
    return int(min(VMEM_LIMIT_CAP, nbytes))


def _rms(x, g):
    ms = jnp.mean(x * x, axis=-1, keepdims=True)
    return x * lax.rsqrt(ms + EPS) * g


def _rope128(x, cos2, sin2):
    return x * cos2 + pltpu.roll(x, 64, 1) * sin2


def _inproj_kernel(x_ref, g_ref, w_ref, cos_ref, sin_ref, *rest, ncast):
    o_ref = rest[ncast]
    for src_ref, dst_ref in zip(rest[:ncast], rest[ncast + 1:]):
        dst_ref[...] = src_ref[...].astype(dst_ref.dtype)
    tn = RET_WIDTH
    u = _rms(x_ref[...], g_ref[...]).astype(BF16)
    cos2 = cos_ref[...]
    sin2 = sin_ref[...]

    def rope_heads(acc, scale):
        return jnp.concatenate(
            [_rope128(acc[:, h * LANES:(h + 1) * LANES], cos2, sin2) * scale
             for h in range(RET_HEADS)], axis=1)

    epilogues = {
        0: lambda acc: rope_heads(acc, RET_HEAD_DIM ** -0.5),
        1: lambda acc: rope_heads(acc, 1.0),
        3: lambda acc: acc * (1.0 / (1.0 + jnp.exp(-acc))),
    }
    for j in range(w_ref.shape[1] // tn):
        acc = jnp.dot(u, w_ref[:, j * tn:(j + 1) * tn], preferred_element_type=F32)
        if j in epilogues:
            acc = epilogues[j](acc)
        o_ref[:, j * tn:(j + 1) * tn] = acc.astype(o_ref.dtype)


def _inproj(x, g, w, cos2, sin2, tm, cast=()):
    m, d = x.shape
    n = w.shape[1]
    steps = m // tm
    assert n == 5 * RET_WIDTH and cos2.shape[0] % tm == 0
    assert all(c.shape[0] % (16 * steps) == 0 for c in cast)
    seq_tiles = cos2.shape[0] // tm
    slabs = [(c.shape[0] // steps, c.shape[1]) for c in cast]
    est = (2 * tm * d * 4 + tm * d * 2 + d * n * 2 + 2 * tm * n * 2 + 3 * tm * RET_WIDTH * 4
           + tm * d * 4 + 4 * tm * LANES * 4 + sum(2 * r * c * 6 for r, c in slabs))
    outs = pl.pallas_call(
        functools.partial(_inproj_kernel, ncast=len(cast)),
        grid=(steps,),
        in_specs=[
            pl.BlockSpec((tm, d), lambda i: (i, 0)),
            pl.BlockSpec((1, d), lambda i: (0, 0)),
            pl.BlockSpec((d, n), lambda i: (0, 0), pipeline_mode=pl.Buffered(1)),
            pl.BlockSpec((tm, LANES), lambda i: (i % seq_tiles, 0)),
            pl.BlockSpec((tm, LANES), lambda i: (i % seq_tiles, 0)),
        ] + [pl.BlockSpec(s, lambda i: (i, 0)) for s in slabs],
        out_specs=[pl.BlockSpec((tm, n), lambda i: (i, 0))]
        + [pl.BlockSpec(s, lambda i: (i, 0)) for s in slabs],
        out_shape=[jax.ShapeDtypeStruct((m, n), BF16)]
        + [jax.ShapeDtypeStruct(c.shape, BF16) for c in cast],
        compiler_params=pltpu.CompilerParams(
            dimension_semantics=("arbitrary",),
            vmem_limit_bytes=_vmem_limit(est + (8 << 20))),
        name="inproj",
    )(x, g, w, cos2, sin2, *cast)
    return outs


def _retention_kernel(q_ref, k_ref, v_ref, km_ref, vm_ref,
                      o_ref,
                      state_ref, d_ref, wq_ref, wk_ref, *, T):
    b = pl.program_id(0)
    t = pl.program_id(1)
    H = RET_HEADS
    dk = RET_HEAD_DIM

    def log_gamma(h):
        e = jnp.full((1, 1), -5.0 - h, F32)
        return jnp.log(1.0 - jnp.exp2(e))

    @pl.when((b == 0) & (t == 0))
    def _():
        ii = lax.broadcasted_iota(jnp.int32, (T, T), 0)
        jj = lax.broadcasted_iota(jnp.int32, (T, T), 1)
        dist = jnp.abs(ii - jj).astype(F32)
        vis = (jj // CHUNK) <= (ii // CHUNK)
        row = lax.broadcasted_iota(jnp.int32, (T, LANES), 0).astype(F32)
        for h in range(H):
            lg = log_gamma(h)
            d_ref[h] = jnp.where(vis, jnp.exp(lg * dist), 0.0)
            wq_ref[h] = jnp.exp(lg * (row + 1.0))
            wk_ref[h] = jnp.exp(lg * (T - 1.0 - row))

    @pl.when(t == 0)
    def _():
        rowm = lax.broadcasted_iota(jnp.int32, (N_META, LANES), 0).astype(F32)
        for h in range(H):
            lg = log_gamma(h)
            km = km_ref[:, h * dk:(h + 1) * dk].astype(F32) * jnp.exp(lg * (N_META - 1.0 - rowm))
            vm = vm_ref[:, h * dk:(h + 1) * dk]
            state_ref[h] = lax.dot_general(
                km.astype(BF16), vm, (((0,), (0,)), ((), ())),
                preferred_element_type=F32)

    pre = []
    for h in range(H):
        cs = slice(h * dk, (h + 1) * dk)
        lg = log_gamma(h)
        q = q_ref[0, :, cs]
        k = k_ref[0, :, cs]
        v = v_ref[0, :, cs]
        s = lax.dot_general(q, k, (((1,), (1,)), ((), ())), preferred_element_type=F32)
        st = state_ref[h]
        inter = jnp.dot((q.astype(F32) * wq_ref[h]).astype(BF16), st.astype(BF16),
                        preferred_element_type=F32)
        kv = lax.dot_general((k.astype(F32) * wk_ref[h]).astype(BF16), v,
                             (((0,), (0,)), ((), ())), preferred_element_type=F32)
        state_ref[h] = st * jnp.exp(lg * float(T)) + kv
        pre.append((s, inter))
    for h in range(H):
        cs = slice(h * dk, (h + 1) * dk)
        s, inter = pre[h]
        sd = (s * d_ref[h]).astype(BF16)
        o = jnp.dot(sd, v_ref[0, :, cs], preferred_element_type=F32) + inter
        mu = jnp.mean(o, axis=-1, keepdims=True)
        oc = o - mu
        var = jnp.mean(oc * oc, axis=-1, keepdims=True)
        o_ref[0, :, cs] = (oc * lax.rsqrt(var + EPS)).astype(o_ref.dtype)


def _retention(projv, proj_meta, T):
    B, S, _ = projv.shape
    W = RET_WIDTH
    est = (2 * 4 * T * W * 2 + 2 * T * W * 2
           + RET_HEADS * (T * T + 2 * T * LANES + 128 * 128) * 4 + 16 * T * T * 4)
    kern = functools.partial(_retention_kernel, T=T)
    return pl.pallas_call(
        kern,
        grid=(B, S // T),
        in_specs=[
            pl.BlockSpec((1, T, W), lambda b, t: (b, t, 0)),
            pl.BlockSpec((1, T, W), lambda b, t: (b, t, 1)),
            pl.BlockSpec((1, T, W), lambda b, t: (b, t, 2)),
            pl.BlockSpec((N_META, W), lambda b, t: (0, 1)),
            pl.BlockSpec((N_META, W), lambda b, t: (0, 2)),
        ],
        out_specs=pl.BlockSpec((1, T, W), lambda b, t: (b, t, 0)),
        out_shape=jax.ShapeDtypeStruct((B, S, W), BF16),
        scratch_shapes=[
            pltpu.VMEM((RET_HEADS, RET_HEAD_DIM, RET_HEAD_DIM), F32),
            pltpu.VMEM((RET_HEADS, T, T), F32),
            pltpu.VMEM((RET_HEADS, T, LANES), F32),
            pltpu.VMEM((RET_HEADS, T, LANES), F32),
        ],
        compiler_params=pltpu.CompilerParams(
            dimension_semantics=("arbitrary", "arbitrary"),
            vmem_limit_bytes=_vmem_limit(est + (8 << 20))),
        name="retention",
    )(projv, projv, projv, proj_meta, proj_meta)


def _swap32(x):
    lane = lax.broadcasted_iota(jnp.int32, x.shape, 1)
    return jnp.where((lane % 64) < 32, pltpu.roll(x, 96, 1), pltpu.roll(x, 32, 1))


def _rope64x2(x, pc, ps):
    return x * pc + _swap32(x) * ps


def _mla_kernel(cq_ref, kv_ref, kvm_ref, pc_ref, ps_ref, pcm_ref, psm_ref,
                qg_ref, kvg_ref, wuq_ref, wukt_ref, sel_ref, wuv_ref, og_ref,
                o_ref,
                kaug_ref, kt_ref, qh_ref, p_ref, m_ref, l_ref, a_ref, acc_ref,
                *, TQ, TK, G):
    i = pl.program_id(1)
    H = MLA_HEADS
    NU = kt_ref.shape[0]
    qscale = (MLA_QK_DIM ** -0.5) * LOG2E
    NEG = -1e30

    def prep_keys(kv, pc, ps):
        c = _rms(kv[:, :KV_LORA], kvg_ref[...])
        kr = _rope64x2(kv[:, KV_LORA:], pc, ps)
        return c.astype(BF16), kr.astype(BF16)

    @pl.when(i == 0)
    def _():
        kaug_ref[0:TK, :] = jnp.zeros((TK, KAUG), BF16)
        cm, krm = prep_keys(kvm_ref[...].astype(F32), pcm_ref[...], psm_ref[...])
        kaug_ref[0:N_META, 0:KV_LORA] = cm
        kaug_ref[0:N_META, KV_LORA:] = krm
        c, kr = prep_keys(kv_ref[0].astype(F32), pc_ref[...], ps_ref[...])
        kaug_ref[TK:, 0:KV_LORA] = c
        kaug_ref[TK:, KV_LORA:] = kr
        nt = (((1,), (1,)), ((), ()))
        for u in range(NU):
            ku = kaug_ref[u * TK:(u + 1) * TK, :]
            kn = lax.dot_general(wukt_ref[...], ku[:, 0:KV_LORA], nt,
                                 preferred_element_type=F32).astype(BF16)
            krt = lax.dot_general(sel_ref[...], ku[:, KV_LORA:], nt,
                                  preferred_element_type=F32).astype(BF16)
            for h in range(H):
                kt_ref[u, h * HEAD_PAD:h * HEAD_PAD + LANES, :] = kn[h * LANES:(h + 1) * LANES]
                kt_ref[u, h * HEAD_PAD + LANES:(h + 1) * HEAD_PAD, :] = krt

        QC = 2 * TQ
        for c in range(cq_ref.shape[1] // QC):
            rows = slice(c * QC, (c + 1) * QC)
            cq = _rms(cq_ref[0, rows, :].astype(F32), qg_ref[...]).astype(BF16)
            q = jnp.dot(cq, wuq_ref[...], preferred_element_type=F32)
            pcq = pc_ref[rows, :]
            psq = ps_ref[rows, :]
            for h in range(H):
                c0 = h * HEAD_PAD
                qn = (q[:, c0:c0 + LANES] * qscale).astype(BF16)
                rp = (_rope64x2(q[:, c0 + LANES:c0 + HEAD_PAD], pcq, psq) * qscale).astype(BF16)
                for sb in range(QC // TQ):
                    qb = c * (QC // TQ) + sb
                    qh_ref[qb, h * TQ:(h + 1) * TQ, 0:LANES] = qn[sb * TQ:(sb + 1) * TQ]
                    qh_ref[qb, h * TQ:(h + 1) * TQ, LANES:] = rp[sb * TQ:(sb + 1) * TQ]

    def block(units, bias, first):
        W = sum(w for _, w in units)
        def key_rows(u, w):
            if isinstance(u, int):
                return kaug_ref[u * TK:u * TK + w, 0:KV_LORA]
            return kaug_ref[pl.ds(pl.multiple_of(u * TK, TK), w), 0:KV_LORA]

        vb = jnp.concatenate([key_rows(u, w) for u, w in units], axis=0)
        HG = H // G
        AHEAD = 2

        def scores(g):
            out = []
            for h in range(g * HG, (g + 1) * HG):
                qh = qh_ref[i, h * TQ:(h + 1) * TQ, :]
                out.append([jnp.dot(qh, kt_ref[u, h * HEAD_PAD:(h + 1) * HEAD_PAD, 0:w],
                                    preferred_element_type=F32) for u, w in units])
            return out

        s_groups = {g: scores(g) for g in range(min(AHEAD, G))}
        for g in range(G):
            gs = slice(g * HG * TQ, (g + 1) * HG * TQ)
            s_g = s_groups.pop(g)
            for hh in range(HG):
                h = g * HG + hh
                rs = slice(h * TQ, (h + 1) * TQ)
                cols = []
                for ui, s in enumerate(s_g[hh]):
                    if bias is not None and bias[ui] is not None:
                        s = s + bias[ui]
                    cols += [s[:, c * LANES:(c + 1) * LANES] for c in range(s.shape[1] // LANES)]
                m_cur = jnp.max(functools.reduce(jnp.maximum, cols), axis=-1, keepdims=True)
                if first:
                    m_new = jnp.broadcast_to(m_cur, (TQ, LANES))
                else:
                    m_prev = m_ref[rs]
                    m_new = jnp.maximum(m_prev, m_cur)
                    alpha = jnp.exp2(m_prev - m_new)
                    a_ref[rs] = alpha
                ps = [jnp.exp2(c - m_new) for c in cols]
                psum = functools.reduce(lambda x, y: x + y, ps)
                if first:
                    l_ref[rs] = psum
                else:
                    l_ref[rs] = alpha * l_ref[rs] + psum
                m_ref[rs] = m_new
                for c, pc_ in enumerate(ps):
                    p_ref[rs, c * LANES:(c + 1) * LANES] = pc_.astype(BF16)
            o = jnp.dot(p_ref[gs, 0:W], vb, preferred_element_type=F32)
            if first:
                acc_ref[gs, :] = o
            else:
                a = a_ref[gs, :]
                acc_ref[gs, 0:LANES] = a * acc_ref[gs, 0:LANES] + o[:, 0:LANES]
                acc_ref[gs, LANES:] = a * acc_ref[gs, LANES:] + o[:, LANES:]
            if g + AHEAD < G:
                s_groups[g + AHEAD] = scores(g + AHEAD)

    colh = lax.broadcasted_iota(jnp.int32, (TQ, TK), 1)
    bias_h = jnp.where(lax.broadcasted_iota(jnp.int32, (TQ, LANES), 1) < N_META, 0.0, NEG).astype(F32)
    rq = lax.broadcasted_iota(jnp.int32, (TQ, TK), 0) // CHUNK
    bias_d = jnp.where(colh // CHUNK <= rq, 0.0, NEG).astype(F32)

    @pl.when(i % 2 == 0)
    def _():
        block([(0, LANES), (i + 1, TK)], [bias_h, bias_d], True)

    @pl.when(i % 2 == 1)
    def _():
        block([(0, LANES), (i + 1, TK), (i, TK)], [bias_h, bias_d, None], True)

    def body(t, carry):
        block([(2 * t + 1, TK), (2 * t + 2, TK)], None, False)
        return carry

    lax.fori_loop(0, i // 2, body, 0)

    outs = []
    for h in range(H):
        rs = slice(h * TQ, (h + 1) * TQ)
        inv_l = 1.0 / jnp.sum(l_ref[rs], axis=-1, keepdims=True)
        o = (acc_ref[rs] * inv_l).astype(BF16)
        outs.append(jnp.dot(o, wuv_ref[h], preferred_element_type=F32))
    mla = jnp.concatenate(outs, axis=-1)
    o_ref[0] = _rms(mla, og_ref[...]).astype(o_ref.dtype)


def _mla(projv, proj_meta, pc, ps, pcm, psm, qg, kvg, wuq, wukt, sel, wuv, og, TQ, G):
    B, S, _ = projv.shape
    TK = TQ
    NU = 1 + S // TK
    R = MLA_HEADS * TQ
    nq = wuq.shape[1]
    HP = MLA_HEADS * HEAD_PAD
    est = (2 * S * Q_LORA * 2 + 2 * TQ * nq * 4 * 2 + 2 * S * KAUG * 2 + 4 * S * LANES * 4
           + 2 * (Q_LORA * nq + 2 * MLA_HEADS * 128 * KV_LORA + LANES * LANES) * 2 + 2 * TQ * MLA_WIDTH * 2
           + NU * TK * KAUG * 2 + NU * HP * TK * 2 + (S // TQ) * R * HEAD_PAD * 2 + R * 3 * TK * 2
           + 3 * R * LANES * 4 + R * KV_LORA * 4
           + 3 * R * TK * 4 + S * KAUG * 4 * 2)
    kern = functools.partial(_mla_kernel, TQ=TQ, TK=TK, G=G)
    const2 = lambda b, i: (0, 0)
    const3 = lambda b, i: (0, 0, 0)
    return pl.pallas_call(
        kern,
        grid=(B, S // TQ),
        in_specs=[
            pl.BlockSpec((1, S, Q_LORA), lambda b, i: (b, 0, COL_CQ // Q_LORA)),
            pl.BlockSpec((1, S, KAUG), lambda b, i: (b, 0, COL_CKV // KAUG)),
            pl.BlockSpec((N_META, KAUG), lambda b, i: (0, COL_CKV // KAUG)),
            pl.BlockSpec((S, LANES), const2),
            pl.BlockSpec((S, LANES), const2),
            pl.BlockSpec((N_META, LANES), const2),
            pl.BlockSpec((N_META, LANES), const2),
            pl.BlockSpec((1, Q_LORA), const2),
            pl.BlockSpec((1, KV_LORA), const2),
            pl.BlockSpec((Q_LORA, nq), const2),
            pl.BlockSpec((MLA_HEADS * MLA_NOPE_DIM, KV_LORA), const2),
            pl.BlockSpec((LANES, LANES), const2),
            pl.BlockSpec((MLA_HEADS, KV_LORA, MLA_V_DIM), const3),
            pl.BlockSpec((1, MLA_WIDTH), const2),
        ],
        out_specs=pl.BlockSpec((1, TQ, MLA_WIDTH), lambda b, i: (b, i, 0)),
        out_shape=jax.ShapeDtypeStruct((B, S, MLA_WIDTH), BF16),
        scratch_shapes=[
            pltpu.VMEM((NU * TK, KAUG), BF16),
            pltpu.VMEM((NU, HP, TK), BF16),
            pltpu.VMEM((S // TQ, R, HEAD_PAD), BF16),
            pltpu.VMEM((R, 3 * TK), BF16),
            pltpu.VMEM((R, LANES), F32),
            pltpu.VMEM((R, LANES), F32),
            pltpu.VMEM((R, LANES), F32),
            pltpu.VMEM((R, KV_LORA), F32),
        ],
        compiler_params=pltpu.CompilerParams(
            dimension_semantics=("arbitrary", "arbitrary"),
            vmem_limit_bytes=_vmem_limit(est + (8 << 20))),
        name="mla",
    )(projv, projv, proj_meta, pc, ps, pcm, psm, qg, kvg, wuq, wukt, sel, wuv, og)


def _oproj_kernel(x_ref, ret_ref, gate_ref, a2_ref, w1_ref, w2_ref, gout_ref, o_ref):
    acc = jnp.dot(a2_ref[...], w2_ref[...], preferred_element_type=F32)
    a1 = (gate_ref[...].astype(F32) * (ret_ref[...].astype(F32) * gout_ref[...])).astype(BF16)
    acc = acc + jnp.dot(a1, w1_ref[...], preferred_element_type=F32)
    o_ref[...] = x_ref[...] + acc


def _oproj(x, ret, proj, a2, w, gout, tm):
    m, d = x.shape
    k = ret.shape[1]
    est = (2 * tm * d * 4 * 2 + 3 * 2 * tm * k * 2 + 2 * k * d * 2 + tm * d * 4
           + 3 * tm * k * 4)
    resident = pl.Buffered(1)
    return pl.pallas_call(
        _oproj_kernel,
        grid=(m // tm,),
        in_specs=[
            pl.BlockSpec((tm, d), lambda i: (i, 0)),
            pl.BlockSpec((tm, k), lambda i: (i, 0)),
            pl.BlockSpec((tm, k), lambda i: (i, 3)),
            pl.BlockSpec((tm, k), lambda i: (i, 0)),
            pl.BlockSpec((k, d), lambda i: (0, 0), pipeline_mode=resident),
            pl.BlockSpec((k, d), lambda i: (1, 0), pipeline_mode=resident),
            pl.BlockSpec((1, k), lambda i: (0, 0)),
        ],
        out_specs=pl.BlockSpec((tm, d), lambda i: (i, 0)),
        out_shape=jax.ShapeDtypeStruct((m, d), F32),
        compiler_params=pltpu.CompilerParams(
            dimension_semantics=("arbitrary",),
            vmem_limit_bytes=_vmem_limit(est + (8 << 20))),
        name="oproj",
    )(x, ret, proj, a2, w, w, gout)


def _mlp_kernel(h_ref, g_ref, wup_ref, wdn_ref, fg_ref, o_ref, m_ref, *, nchunks, nsplit):
    f = pl.program_id(1)
    last = pl.num_programs(1) - 1
    rc = m_ref.shape[0] // nchunks

    def ffn(m):
        fs = wup_ref.shape[1] // nsplit
        ups = [jnp.dot(m, wup_ref[:, s * fs:(s + 1) * fs], preferred_element_type=F32)
               for s in range(nsplit)]
        out = None
        for s in range(nsplit):
            a = jnp.square(jnp.maximum(ups[s], 0.0)).astype(BF16)
            d = jnp.dot(a, wdn_ref[s * fs:(s + 1) * fs, :], preferred_element_type=F32)
            out = d if out is None else out + d
        return out

    @pl.when(f == 0)
    def _():
        m = _rms(h_ref[...], g_ref[...]).astype(BF16)
        m_ref[...] = m
        o_ref[...] = ffn(m)

    @pl.when((f > 0) & (f < last))
    def _():
        o_ref[...] += ffn(m_ref[...])

    @pl.when(f == last)
    def _():
        for c in range(nchunks):
            rows = slice(c * rc, (c + 1) * rc)
            y = o_ref[rows, :] + ffn(m_ref[rows, :]) + h_ref[rows, :]
            o_ref[rows, :] = _rms(y, fg_ref[...])


def _mlp(h, g, wup, wdn, fg, tm, tf, nchunks, nsplit):
    m, d = h.shape
    ff = wup.shape[1]
    est = 2 * tm * d * 4 * 2 + tm * d * 2 + 2 * 2 * d * tf * 2 + tm * tf * 6
    kern = functools.partial(_mlp_kernel, nchunks=nchunks, nsplit=nsplit)
    return pl.pallas_call(
        kern,
        grid=(m // tm, ff // tf),
        in_specs=[
            pl.BlockSpec((tm, d), lambda i, f: (i, 0)),
            pl.BlockSpec((1, d), lambda i, f: (0, 0)),
            pl.BlockSpec((d, tf), lambda i, f: (0, f)),
            pl.BlockSpec((tf, d), lambda i, f: (f, 0)),
            pl.BlockSpec((1, d), lambda i, f: (0, 0)),
        ],
        out_specs=pl.BlockSpec((tm, d), lambda i, f: (i, 0)),
        out_shape=jax.ShapeDtypeStruct((m, d), F32),
        scratch_shapes=[pltpu.VMEM((tm, d), BF16)],
        compiler_params=pltpu.CompilerParams(
            dimension_semantics=("arbitrary", "arbitrary"),
            vmem_limit_bytes=_vmem_limit(est + (8 << 20))),
        name="mlp",
    )(h, g, wup, wdn, fg)


def _rope_tables(pos, dim):
    inv = 1.0 / (ROPE_BASE ** (jnp.arange(0, dim, 2, dtype=F32) / dim))
    ang = pos[:, None] * inv[None, :]
    return jnp.cos(ang), jnp.sin(ang)


def kernel(x, meta_tokens, norm_mix_g, w_in, ret_out_g, q_norm_g, w_uq, kv_norm_g,
           w_uk, w_uv, mla_out_g, w_o, norm_mlp_g, w_up, w_down, final_norm_g):
    B, S, D = x.shape
    assert norm_mix_g.shape[0] == 1 and meta_tokens.shape == (N_META, D)
    M = B * S
    xm = x.reshape(M, D)

    w_in_aug = jnp.pad(w_in[0].astype(BF16), ((0, 0), (0, PROJ_WIDTH - w_in.shape[2])))
    wq3 = w_uq[0].reshape(Q_LORA, MLA_HEADS, MLA_QK_DIM)
    wuq = jnp.concatenate(
        [wq3, jnp.zeros((Q_LORA, MLA_HEADS, HEAD_PAD - MLA_QK_DIM), wq3.dtype)],
        axis=2).reshape(Q_LORA, MLA_HEADS * HEAD_PAD).astype(BF16)
    wukt = w_uk[0].T.astype(BF16)
    sel = jnp.zeros((LANES, LANES), F32).at[:MLA_ROPE_DIM, :MLA_ROPE_DIM].set(
        jnp.eye(MLA_ROPE_DIM, dtype=F32)).astype(BF16)
    wuv = jnp.transpose(w_uv[0].reshape(KV_LORA, MLA_HEADS, MLA_V_DIM), (1, 0, 2)).astype(BF16)

    pos = jnp.arange(S, dtype=F32) + float(N_META)
    posm = jnp.arange(N_META, dtype=F32)

    def ret_tables(p):
        c, s = _rope_tables(p, RET_HEAD_DIM)
        return jnp.concatenate([c, c], 1), jnp.concatenate([-s, s], 1)

    def mla_tables(p):
        c, s = _rope_tables(p, MLA_ROPE_DIM)
        return jnp.concatenate([c, c, c, c], 1), jnp.concatenate([-s, s, -s, s], 1)

    cos2, sin2 = ret_tables(pos)
    cosm, sinm = ret_tables(posm)
    pc, ps = mla_tables(pos)
    pcm, psm = mla_tables(posm)

    g_mix = norm_mix_g[0].reshape(1, D)
    proj, wup, wdn, wo = _inproj(xm, g_mix, w_in_aug, cos2, sin2, tm=512,
                                 cast=(w_up[0], w_down[0], w_o[0]))
    proj_meta, = _inproj(meta_tokens, g_mix, w_in_aug, cosm, sinm, tm=N_META)
    projv = proj.reshape(B, S, PROJ_WIDTH)

    ret = _retention(projv, proj_meta, T=256)
    mla = _mla(projv, proj_meta, pc, ps, pcm, psm,
               q_norm_g[0].reshape(1, Q_LORA), kv_norm_g[0].reshape(1, KV_LORA),
               wuq, wukt, sel, wuv, mla_out_g[0].reshape(1, MLA_WIDTH), TQ=256, G=8)

    h = _oproj(xm, ret.reshape(M, RET_WIDTH), proj, mla.reshape(M, MLA_WIDTH), wo,
               ret_out_g[0].reshape(1, RET_WIDTH), tm=1024)
    out = _mlp(h, norm_mlp_g[0].reshape(1, D), wup, wdn, final_norm_g.reshape(1, D),
               tm=1024, tf=1024, nchunks=2, nsplit=2)
    return out.reshape(B, S, D)
```
